```python
import math
import jax, jax.numpy as jnp
from jax import lax
import numpy as np

D_MODEL = 1024
BATCH = 4
SEQ = 4096
DEPTH = 4
DEC_BATCH = 32
DEC_SEQ = 4
PAST_LEN = 8192
PAGE_SIZE = 128

N_A_LAYERS = DEPTH // 2
N_B_LAYERS = DEPTH - N_A_LAYERS
RET_HEADS = 4
RET_DK = D_MODEL // RET_HEADS
RET_DV = 2 * D_MODEL // RET_HEADS
RET_CHUNK = 128
ROPE_BASE = 10000.0
FOX_HEADS = 16
FOX_HD = D_MODEL // FOX_HEADS
Q_BLOCK = 128
D_FF = -(-8 * D_MODEL // (3 * 256)) * 256
EPS = 1e-6
FORGET_BIAS_INIT = 6.0

kernel_name = 'yoco_retention_fox_decoder_step'


def _rmsnorm(x, g):
    xf = x.astype(jnp.float32)
    y = xf * lax.rsqrt(jnp.mean(xf * xf, axis=-1, keepdims=True) + EPS)
    return (y * g.astype(jnp.float32)).astype(x.dtype)


def _modulate(h, shift, scale):
    return h * (1 + scale[:, None, :]) + shift[:, None, :]


def _rotary(x, pos):
    half = x.shape[-1] // 2
    freqs = ROPE_BASE ** (-jnp.arange(half, dtype=jnp.float32) / half)
    ang = pos.astype(jnp.float32)[:, None] * freqs[None, :]
    cos = jnp.cos(ang)[None, :, None, :]
    sin = jnp.sin(ang)[None, :, None, :]
    xf = x.astype(jnp.float32)
    x1, x2 = xf[..., :half], xf[..., half:]
    return jnp.concatenate([x1 * cos - x2 * sin, x1 * sin + x2 * cos], axis=-1).astype(x.dtype)


def _retention_scan(q, k, v, s0):
    B, T, H, DK = q.shape
    DV = v.shape[-1]
    C = math.gcd(T, RET_CHUNK)
    n = T // C
    log_g = jnp.log1p(-jnp.exp2(-5.0 - jnp.arange(H, dtype=jnp.float32)))
    i = jnp.arange(C, dtype=jnp.float32)
    diff = i[:, None] - i[None, :]
    inner = jnp.where(diff[None] >= 0, jnp.exp(log_g[:, None, None] * jnp.maximum(diff, 0.0)[None]), 0.0)
    cross = jnp.exp(log_g[None, :] * (i[:, None] + 1.0))
    kdec = jnp.exp(log_g[None, :] * (C - 1.0 - i)[:, None])
    cdec = jnp.exp(log_g * C)

    def to_chunks(a):
        return a.astype(jnp.float32).reshape(B, n, C, H, a.shape[-1]).transpose(1, 0, 2, 3, 4)

    def step(S, inp):
        qc, kc, vc = inp
        sc = jnp.einsum('bihd,bjhd->bhij', qc, kc) * inner[None]
        o = (jnp.einsum('bhij,bjhe->bihe', sc, vc)
             + jnp.einsum('bihd,bhde->bihe', qc, S) * cross[None, :, :, None])
        S = S * cdec[None, :, None, None] + jnp.einsum('bjhd,bjhe->bhde', kc * kdec[None, :, :, None], vc)
        return S, o

    S, o = lax.scan(step, s0.astype(jnp.float32), (to_chunks(q), to_chunks(k), to_chunks(v)))
    o = o.transpose(1, 0, 2, 3, 4).reshape(B, T, H, DV)
    return o, S


def _retention_mixer(h, pos, s0, w_in, w_out):
    B, T, _ = h.shape
    proj = h @ w_in
    q, k, v, g = jnp.split(proj, [D_MODEL, 2 * D_MODEL, 4 * D_MODEL], axis=-1)
    q = _rotary(q.reshape(B, T, RET_HEADS, RET_DK), pos)
    k = _rotary(k.reshape(B, T, RET_HEADS, RET_DK), pos) * (RET_DK ** -0.5)
    v = v.reshape(B, T, RET_HEADS, RET_DV)
    o, S = _retention_scan(q, k, v, s0)
    o = o * lax.rsqrt(jnp.mean(o * o, axis=-1, keepdims=True) + EPS)
    o = o.reshape(B, T, RET_HEADS * RET_DV).astype(h.dtype) * jax.nn.silu(g)
    return o @ w_out, S.astype(s0.dtype)


def _shared_kvf(x, c_act, kv_norm_g, w_kvmod, b_kvmod, w_kvf, b_f, k_norm_g):
    B, T, _ = x.shape
    shift, scale = jnp.split(c_act @ w_kvmod + b_kvmod, 2, axis=-1)
    h = _modulate(_rmsnorm(x, kv_norm_g), shift, scale)
    k, v, fl = jnp.split(h @ w_kvf, [D_MODEL, 2 * D_MODEL], axis=-1)
    k = _rmsnorm(k.reshape(B, T, FOX_HEADS, FOX_HD), k_norm_g)
    v = v.reshape(B, T, FOX_HEADS, FOX_HD)
    logf = jax.nn.log_sigmoid((fl + b_f).astype(jnp.float32)).astype(x.dtype)
    return k, v, logf


def _fox_prompt(q, k, v, logf):
    B, T, H, HD = q.shape
    nb = T // Q_BLOCK
    scale = HD ** -0.5
    F = jnp.cumsum(logf.astype(jnp.float32), axis=1).transpose(0, 2, 1)
    kpos = jnp.arange(T)
    qb = q.reshape(B, nb, Q_BLOCK, H, HD).transpose(1, 0, 2, 3, 4)
    Fb = F.reshape(B, H, nb, Q_BLOCK).transpose(2, 0, 1, 3)
    pb = kpos.reshape(nb, Q_BLOCK)

    def block(args):
        qi, Fi, pi = args
        s = (jnp.einsum('bqhd,bkhd->bhqk', qi, k).astype(jnp.float32) * scale
             + Fi[..., None] - F[:, :, None, :])
        s = jnp.where(pi[:, None] >= kpos[None, :], s, -jnp.inf)
        p = jax.nn.softmax(s, axis=-1).astype(v.dtype)
        return jnp.einsum('bhqk,bkhd->bqhd', p, v)

    o = lax.map(block, (qb, Fb, pb))
    return o.transpose(1, 0, 2, 3, 4).reshape(B, T, H, HD)


def _fox_sample(q, k, v, logf, pk, pv, R):
    T = q.shape[1]
    P = pk.shape[1]
    scale = q.shape[-1] ** -0.5
    Fn = jnp.cumsum(logf.astype(jnp.float32), axis=1).transpose(0, 2, 1)
    s_past = (jnp.einsum('bqhd,bkhd->bhqk', q, pk).astype(jnp.float32) * scale
              + Fn[..., None] + R[:, :, None, :])
    s_new = (jnp.einsum('bqhd,bkhd->bhqk', q, k).astype(jnp.float32) * scale
             + Fn[..., None] - Fn[:, :, None, :])
    t = jnp.arange(T)
    s_new = jnp.where(t[:, None] >= t[None, :], s_new, -jnp.inf)
    p = jax.nn.softmax(jnp.concatenate([s_past, s_new], axis=-1), axis=-1).astype(v.dtype)
    return (jnp.einsum('bhqk,bkhd->bqhd', p[..., :P], pv)
            + jnp.einsum('bhqk,bkhd->bqhd', p[..., P:], v))


def _trunk(x, c, pos, ret_s0, past, w_mod, b_mod, ret_norm_g, w_ret_in, w_ret_out,
           attn_norm_g, w_q, q_norm_g, w_o, ffn_norm_g, w_ffn_in, w_ffn_out,
           kv_norm_g, w_kvmod, b_kvmod, w_kvf, b_f, k_norm_g):
    B, T, _ = x.shape
    c_act = jax.nn.silu(c)
    ret_states = []
    shared = None
    for l in range(DEPTH):
        sh1, sc1, g1, sh2, sc2, g2 = jnp.split(c_act @ w_mod[l] + b_mod[l], 6, axis=-1)
        if l < N_A_LAYERS:
            h = _modulate(_rmsnorm(x, ret_norm_g[l]), sh1, sc1)
            out, S = _retention_mixer(h, pos, ret_s0[l], w_ret_in[l], w_ret_out[l])
            ret_states.append(S)
        else:
            if shared is None:
                shared = _shared_kvf(x, c_act, kv_norm_g, w_kvmod, b_kvmod, w_kvf, b_f, k_norm_g)
            k, v, logf = shared
            j = l - N_A_LAYERS
            h = _modulate(_rmsnorm(x, attn_norm_g[j]), sh1, sc1)
            q = _rmsnorm((h @ w_q[j]).reshape(B, T, FOX_HEADS, FOX_HD), q_norm_g[j])
            if past is None:
                o = _fox_prompt(q, k, v, logf)
            else:
                o = _fox_sample(q, k, v, logf, past[0], past[1], past[2])
            out = o.reshape(B, T, D_MODEL) @ w_o[j]
        x = x + (1 + g1[:, None, :]) * out
        h = _modulate(_rmsnorm(x, ffn_norm_g[l]), sh2, sc2)
        gate, up = jnp.split(h @ w_ffn_in[l], 2, axis=-1)
        x = x + (1 + g2[:, None, :]) * ((jax.nn.silu(gate) * up) @ w_ffn_out[l])
    k, v, logf = shared
    return x, jnp.stack(ret_states), k, v, logf


def setup_inputs(seed: int = 0) -> dict:
    key = jax.random.key(seed)
    ks = jax.random.split(key, 32)
    f32 = jnp.float32

    def nrm(k, shape, scale):
        return jax.random.normal(k, shape, f32) * scale

    n_pages = PAST_LEN // PAGE_SIZE
    n_used = DEC_BATCH * n_pages
    n_pool = n_used + max(1, n_used // 4)
    perm = jax.random.permutation(ks[0], n_pool)
    page_table = perm[:n_used].reshape(DEC_BATCH, n_pages).astype(jnp.int32)
    D = D_MODEL
    return {
        'x_prompt': nrm(ks[1], (BATCH, SEQ, D), 1.0),
        'x_sample': nrm(ks[2], (DEC_BATCH, DEC_SEQ, D), 1.0),
        'cache_k': nrm(ks[3], (n_pool, PAGE_SIZE, FOX_HEADS, FOX_HD), 1.0),
        'cache_v': nrm(ks[4], (n_pool, PAGE_SIZE, FOX_HEADS, FOX_HD), 1.0),
        'cache_logf': jax.nn.log_sigmoid(FORGET_BIAS_INIT + 1.5 + nrm(ks[5], (n_pool, PAGE_SIZE, FOX_HEADS), 0.5)),
        'state_ret': nrm(ks[6], (N_A_LAYERS, DEC_BATCH, RET_HEADS, RET_DK, RET_DV), 0.3),
        'page_table': page_table,
        'c_prompt': nrm(ks[7], (BATCH, D), 1.0),
        'c_sample': nrm(ks[8], (DEC_BATCH, D), 1.0),
        'w_mod': nrm(ks[9], (DEPTH, D, 6 * D), 0.5 * D ** -0.5),
        'b_mod': nrm(ks[10], (DEPTH, 6 * D), 0.01),
        'ret_norm_g': 1.0 + nrm(ks[11], (N_A_LAYERS, D), 0.02),
        'w_ret_in': nrm(ks[12], (N_A_LAYERS, D, 6 * D), D ** -0.5),
        'w_ret_out': nrm(ks[13], (N_A_LAYERS, 2 * D, D), (2 * D) ** -0.5),
        'attn_norm_g': 1.0 + nrm(ks[14], (N_B_LAYERS, D), 0.02),
        'w_q': nrm(ks[15], (N_B_LAYERS, D, D), D ** -0.5),
        'q_norm_g': 1.0 + nrm(ks[16], (N_B_LAYERS, FOX_HD), 0.02),
        'w_o': nrm(ks[17], (N_B_LAYERS, D, D), D ** -0.5),
        'ffn_norm_g': 1.0 + nrm(ks[18], (DEPTH, D), 0.02),
        'w_ffn_in': nrm(ks[19], (DEPTH, D, 2 * D_FF), D ** -0.5),
        'w_ffn_out': nrm(ks[20], (DEPTH, D_FF, D), D_FF ** -0.5),
        'kv_norm_g': 1.0 + nrm(ks[21], (D,), 0.02),
        'w_kvmod': nrm(ks[22], (D, 2 * D), 0.5 * D ** -0.5),
        'b_kvmod': nrm(ks[23], (2 * D,), 0.01),
        'w_kvf': nrm(ks[24], (D, 2 * D + FOX_HEADS), D ** -0.5),
        'b_f': FORGET_BIAS_INIT + nrm(ks[25], (FOX_HEADS,), 0.1),
        'k_norm_g': 1.0 + nrm(ks[26], (FOX_HD,), 0.02),
    }


def reference(x_prompt, x_sample, cache_k, cache_v, cache_logf, state_ret, page_table,
              c_prompt, c_sample, w_mod, b_mod, ret_norm_g, w_ret_in, w_ret_out,
              attn_norm_g, w_q, q_norm_g, w_o, ffn_norm_g, w_ffn_in, w_ffn_out,
              kv_norm_g, w_kvmod, b_kvmod, w_kvf, b_f, k_norm_g):
    params = (w_mod, b_mod, ret_norm_g, w_ret_in, w_ret_out, attn_norm_g, w_q, q_norm_g, w_o,
              ffn_norm_g, w_ffn_in, w_ffn_out, kv_norm_g, w_kvmod, b_kvmod, w_kvf, b_f, k_norm_g)
    n_pages = PAST_LEN // PAGE_SIZE
    past_len = n_pages * PAGE_SIZE
    Bp, Tp = x_prompt.shape[:2]
    Bd, Td = x_sample.shape[:2]

    s0_prompt = jnp.zeros((N_A_LAYERS, Bp, RET_HEADS, RET_DK, RET_DV), state_ret.dtype)
    y_prompt, st_p, k_p, v_p, lf_p = _trunk(x_prompt, c_prompt, jnp.arange(Tp), s0_prompt, None, *params)

    pk = cache_k[page_table].reshape(Bd, past_len, FOX_HEADS, FOX_HD)
    pv = cache_v[page_table].reshape(Bd, past_len, FOX_HEADS, FOX_HD)
    plf = cache_logf[page_table].reshape(Bd, past_len, FOX_HEADS).astype(jnp.float32)
    R = (lax.cumsum(plf, axis=1, reverse=True) - plf).transpose(0, 2, 1)
    pos_s = PAST_LEN + jnp.arange(Td)
    y_sample, st_s, k_s, v_s, lf_s = _trunk(x_sample, c_sample, pos_s, state_ret, (pk, pv, R), *params)

    return (y_prompt, y_sample, st_p, st_s, k_p, v_p, lf_p, k_s, v_s, lf_s)
```

```python
import functools
import math

import jax
import jax.numpy as jnp
from jax import lax
from jax.experimental import pallas as pl
from jax.experimental.pallas import tpu as pltpu

BF = jnp.bfloat16
F32 = jnp.float32

D_MODEL = 1024
RET_HEADS = 4
RET_DK = 256
RET_DV = 512
RET_CHUNK = 128
ROPE_BASE = 10000.0
FOX_HEADS = 16
FOX_HD = 64
EPS = 1e-6
LANES = 128
VMEM_LIMIT = 52 * 1024 * 1024
NEG = -1e30

_NT = (((1,), (1,)), ((), ()))
_TN = (((0,), (0,)), ((), ()))


def _cparams(*sem):
    return pltpu.CompilerParams(dimension_semantics=sem, vmem_limit_bytes=VMEM_LIMIT)


def _dot(a, b):
    return jnp.dot(a, b, preferred_element_type=F32)


def _dot_exact(a, b):
    return jnp.dot(a, b, preferred_element_type=F32, precision=lax.Precision.HIGHEST)


def _silu(x):
    return x * jax.nn.sigmoid(x)


def _normmod(x, g, shift, scale):
    ms = jnp.mean(x * x, axis=-1, keepdims=True)
    y = x * lax.rsqrt(ms + EPS) * g
    return y * (1.0 + scale) + shift


def _split_bf16(x):
    hi = x.astype(BF)
    lo = (x - hi.astype(F32)).astype(BF)
    return hi, lo


def _headnorm(a):
    r = lax.broadcasted_iota(jnp.int32, (D_MODEL, LANES), 0) >> 6
    c = lax.broadcasted_iota(jnp.int32, (D_MODEL, LANES), 1)
    seg = jnp.where(r == c, 1.0, 0.0).astype(BF)
    rt = lax.broadcasted_iota(jnp.int32, (LANES, D_MODEL), 0)
    ct = lax.broadcasted_iota(jnp.int32, (LANES, D_MODEL), 1) >> 6
    seg_t = jnp.where(rt == ct, 1.0, 0.0).astype(BF)
    y_hi, y_lo = _split_bf16(a * a)
    ms = (_dot(y_hi, seg) + _dot(y_lo, seg)) * (1.0 / FOX_HD)
    rs = lax.rsqrt(ms + EPS)
    r_hi, r_lo = _split_bf16(rs)
    return a * (_dot(r_hi, seg_t) + _dot(r_lo, seg_t))


def _log_sigmoid(x):
    return jnp.minimum(x, 0.0) - jnp.log1p(jnp.exp(-jnp.abs(x)))


def _cond_kernel(c_ref, w_ref, b_ref, o_ref):
    ca = _silu(c_ref[...]).astype(BF)
    o_ref[...] = _dot(ca, w_ref[...].astype(BF)) + b_ref[...]


def _cond_proj(c, w, b, tn):
    L, K, N = w.shape
    R = c.shape[0]
    return pl.pallas_call(
        _cond_kernel,
        grid=(L, N // tn),
        in_specs=[
            pl.BlockSpec((R, K), lambda l, j: (0, 0)),
            pl.BlockSpec((None, K, tn), lambda l, j: (l, 0, j)),
            pl.BlockSpec((None, 1, tn), lambda l, j: (l, 0, j)),
        ],
        out_specs=pl.BlockSpec((None, R, tn), lambda l, j: (l, 0, j)),
        out_shape=jax.ShapeDtypeStruct((L, R, N), F32),
        compiler_params=_cparams("parallel", "parallel"),
        name="cond_proj",
    )(c, w, b.reshape(L, 1, N))


class _Mod:
    def __init__(self, rows_per_group, per_row):
        self.rows_per_group = rows_per_group
        self.per_row = per_row

    def spec(self, tm):
        if self.per_row:
            return pl.BlockSpec((None, tm, D_MODEL), lambda i, *_: (0, i, 0))
        tiles = self.rows_per_group // tm
        return pl.BlockSpec((None, 1, D_MODEL), lambda i, *_: (i // tiles, 0, 0))


def _vec_spec(n):
    return pl.BlockSpec((1, n), lambda i, *_: (0, 0))


def _nm_kernel(x_ref, g_ref, sh_ref, sc_ref, w_ref, o_ref, h_ref):
    @pl.when(pl.program_id(1) == 0)
    def _():
        h_ref[...] = _normmod(x_ref[...], g_ref[...], sh_ref[...], sc_ref[...]).astype(BF)

    o_ref[...] = _dot(h_ref[...], w_ref[...]).astype(o_ref.dtype)


def _nm_matmul(x, g, shift, scale, w, mod, tm, tn, out_dtype):
    M, K = x.shape
    N = w.shape[1]
    return pl.pallas_call(
        _nm_kernel,
        grid=(M // tm, N // tn),
        in_specs=[
            pl.BlockSpec((tm, K), lambda i, j: (i, 0)),
            _vec_spec(K),
            mod.spec(tm),
            mod.spec(tm),
            pl.BlockSpec((K, tn), lambda i, j: (0, j)),
        ],
        out_specs=pl.BlockSpec((tm, tn), lambda i, j: (i, j)),
        out_shape=jax.ShapeDtypeStruct((M, N), out_dtype),
        scratch_shapes=[pltpu.VMEM((tm, K), BF)],
        compiler_params=_cparams("parallel", "arbitrary"),
        name="norm_mod_matmul",
    )(x, g.reshape(1, K), shift, scale, w)


def _q_kernel(x_ref, g_ref, sh_ref, sc_ref, w_ref, gq_ref, o_ref):
    h = _normmod(x_ref[...], g_ref[...], sh_ref[...], sc_ref[...]).astype(BF)
    q = _headnorm(_dot(h, w_ref[...])) * gq_ref[...]
    o_ref[...] = (q * (FOX_HD ** -0.5)).astype(o_ref.dtype)


def _q_proj(x, g, shift, scale, w, gq, mod, tm, out_dtype):
    M, K = x.shape
    return pl.pallas_call(
        _q_kernel,
        grid=(M // tm,),
        in_specs=[
            pl.BlockSpec((tm, K), lambda i: (i, 0)),
            _vec_spec(K),
            mod.spec(tm),
            mod.spec(tm),
            pl.BlockSpec((K, D_MODEL), lambda i: (0, 0)),
            _vec_spec(D_MODEL),
        ],
        out_specs=pl.BlockSpec((tm, D_MODEL), lambda i: (i, 0)),
        out_shape=jax.ShapeDtypeStruct((M, D_MODEL), out_dtype),
        compiler_params=_cparams("parallel"),
        name="q_proj_headnorm",
    )(x, g.reshape(1, K), shift, scale, w, jnp.tile(gq, FOX_HEADS).reshape(1, D_MODEL))


def _kvf_kernel(x_ref, g_ref, sh_ref, sc_ref, wk_ref, wv_ref, wf_ref, bf_ref, gk_ref,
                k_ref, v_ref, lf_ref, cum_ref, kb_ref, vb_ref, carry_ref, *, tiles_per_seq):
    tm = x_ref.shape[0]
    h = _normmod(x_ref[...], g_ref[...], sh_ref[...], sc_ref[...]).astype(BF)
    k = _headnorm(_dot(h, wk_ref[...])) * gk_ref[...]
    v = _dot(h, wv_ref[...])
    k_ref[...] = k
    v_ref[...] = v
    kb_ref[...] = k.astype(BF)
    vb_ref[...] = v.astype(BF)
    lf = _log_sigmoid(_dot(h, wf_ref[...]) + bf_ref[...])
    lf_ref[...] = lf[:, :FOX_HEADS]

    @pl.when(pl.program_id(0) % tiles_per_seq == 0)
    def _():
        carry_ref[...] = jnp.zeros_like(carry_ref)

    r = lax.broadcasted_iota(jnp.int32, (tm, tm), 0)
    c = lax.broadcasted_iota(jnp.int32, (tm, tm), 1)
    tri = jnp.where(r >= c, 1.0, 0.0)
    cum = _dot_exact(tri, lf) + carry_ref[...]
    carry_ref[...] = cum[tm - 1:tm, :]
    cum_ref[...] = cum[:, :FOX_HEADS]


def _kvf_proj(x, g, shift, scale, wk, wv, wf, b_f, gk, mod, tm, seq_len):
    M, K = x.shape
    tok = pl.BlockSpec((tm, D_MODEL), lambda i: (i, 0))
    hd = pl.BlockSpec((tm, FOX_HEADS), lambda i: (i, 0))
    w_spec = pl.BlockSpec((K, D_MODEL), lambda i: (0, 0))
    bf_pad = jnp.zeros((1, LANES), F32).at[0, :FOX_HEADS].set(b_f)
    return pl.pallas_call(
        functools.partial(_kvf_kernel, tiles_per_seq=max(seq_len // tm, 1)),
        grid=(M // tm,),
        in_specs=[
            pl.BlockSpec((tm, K), lambda i: (i, 0)),
            _vec_spec(K),
            mod.spec(tm),
            mod.spec(tm),
            w_spec,
            w_spec,
            pl.BlockSpec((K, LANES), lambda i: (0, 0)),
            _vec_spec(LANES),
            _vec_spec(D_MODEL),
        ],
        out_specs=[tok, tok, hd, hd, tok, tok],
        out_shape=[
            jax.ShapeDtypeStruct((M, D_MODEL), F32),
            jax.ShapeDtypeStruct((M, D_MODEL), F32),
            jax.ShapeDtypeStruct((M, FOX_HEADS), F32),
            jax.ShapeDtypeStruct((M, FOX_HEADS), F32),
            jax.ShapeDtypeStruct((M, D_MODEL), BF),
            jax.ShapeDtypeStruct((M, D_MODEL), BF),
        ],
        scratch_shapes=[pltpu.VMEM((1, LANES), F32)],
        compiler_params=_cparams("arbitrary"),
        name="shared_kvf",
    )(x, g.reshape(1, K), shift, scale, wk, wv, wf, bf_pad,
      jnp.tile(gk, FOX_HEADS).reshape(1, D_MODEL))


def _out_kernel(a_ref, w_ref, x_ref, gt_ref, o_ref):
    y = _dot(a_ref[...].astype(BF), w_ref[...])
    o_ref[...] = x_ref[...] + (1.0 + gt_ref[...]) * y


def _out_proj(a, w, x, gate, mod, tm):
    M, K = a.shape
    return pl.pallas_call(
        _out_kernel,
        grid=(M // tm,),
        in_specs=[
            pl.BlockSpec((tm, K), lambda i: (i, 0)),
            pl.BlockSpec((K, D_MODEL), lambda i: (0, 0)),
            pl.BlockSpec((tm, D_MODEL), lambda i: (i, 0)),
            mod.spec(tm),
        ],
        out_specs=pl.BlockSpec((tm, D_MODEL), lambda i: (i, 0)),
        out_shape=jax.ShapeDtypeStruct((M, D_MODEL), F32),
        compiler_params=_cparams("parallel"),
        name="out_proj_residual",
    )(a, w, x, gate)


FFN_CHUNK = 256


def _ffn_kernel(x_ref, g_ref, sh_ref, sc_ref, gt_ref, wi_ref, wo_ref, o_ref, h_ref, acc_ref, *, nf):
    h_ref[...] = _normmod(x_ref[...], g_ref[...], sh_ref[...], sc_ref[...]).astype(BF)
    acc_ref[...] = jnp.zeros_like(acc_ref)

    def body(f, carry):
        h = h_ref[...]
        gate = _dot(h, wi_ref[f])
        up = _dot(h, wi_ref[nf + f])
        act = (_silu(gate) * up).astype(BF)
        acc_ref[...] += _dot(act, wo_ref[f])
        return carry

    lax.fori_loop(0, nf, body, 0)
    o_ref[...] = x_ref[...] + (1.0 + gt_ref[...]) * acc_ref[...]


def _ffn(x, g, shift, scale, gate, wi, wo, mod, tm):
    M, K = x.shape
    nf = wo.shape[0]
    resident = pl.Buffered(1)
    return pl.pallas_call(
        functools.partial(_ffn_kernel, nf=nf),
        grid=(M // tm,),
        in_specs=[
            pl.BlockSpec((tm, K), lambda i: (i, 0)),
            _vec_spec(K),
            mod.spec(tm),
            mod.spec(tm),
            mod.spec(tm),
            pl.BlockSpec(wi.shape, lambda i: (0, 0, 0), pipeline_mode=resident),
            pl.BlockSpec(wo.shape, lambda i: (0, 0, 0), pipeline_mode=resident),
        ],
        out_specs=pl.BlockSpec((tm, K), lambda i: (i, 0)),
        out_shape=jax.ShapeDtypeStruct((M, K), F32),
        scratch_shapes=[pltpu.VMEM((tm, K), BF), pltpu.VMEM((tm, K), F32)],
        compiler_params=_cparams("parallel"),
        name="ffn_swiglu",
    )(x, g.reshape(1, K), shift, scale, gate, wi, wo)


def _rotary(x, cos, sin):
    half = x.shape[-1] // 2
    x1, x2 = x[:, :half], x[:, half:]
    return jnp.concatenate([x1 * cos - x2 * sin, x1 * sin + x2 * cos], axis=-1)


def _ret_prompt_kernel(q_ref, k_ref, v_ref, g_ref, cos_ref, sin_ref, lg_ref, o_ref, s_ref, *, rows):
    C = RET_CHUNK

    @pl.when(pl.program_id(2) == 0)
    def _():
        s_ref[...] = jnp.zeros_like(s_ref)

    lg = lg_ref[...]
    i_v = lax.broadcasted_iota(jnp.int32, (C, RET_DV), 0).astype(F32)
    i_k = lax.broadcasted_iota(jnp.int32, (C, RET_DK), 0).astype(F32)
    cross = jnp.exp(lg * (i_v + 1.0))
    kdec = jnp.exp(lg_ref[:, :RET_DK] * (C - 1.0 - i_k))
    cdec = jnp.exp(lg * float(C))
    ii = lax.broadcasted_iota(jnp.int32, (C, C), 0)
    jj = lax.broadcasted_iota(jnp.int32, (C, C), 1)
    diff = (ii - jj).astype(F32)
    inner = jnp.where(diff >= 0, jnp.exp(lg_ref[:, :C] * jnp.maximum(diff, 0.0)), 0.0)

    for c in range(rows // C):
        sl = slice(c * C, (c + 1) * C)
        cos, sin = cos_ref[sl, :], sin_ref[sl, :]
        q = _rotary(q_ref[sl, :].astype(F32), cos, sin)
        k = _rotary(k_ref[sl, :].astype(F32), cos, sin) * (RET_DK ** -0.5)
        v = v_ref[sl, :]
        qb = q.astype(BF)
        S = s_ref[...]
        sc = lax.dot_general(qb, k.astype(BF), _NT, preferred_element_type=F32) * inner
        o = _dot(sc.astype(BF), v) + _dot(qb, S.astype(BF)) * cross
        s_ref[...] = S * cdec + lax.dot_general((k * kdec).astype(BF), v, _TN,
                                                preferred_element_type=F32)
        o = o * lax.rsqrt(jnp.mean(o * o, axis=-1, keepdims=True) + EPS)
        o_ref[sl, :] = (o * _silu(g_ref[sl, :].astype(F32))).astype(o_ref.dtype)


def _ret_prompt(proj, cos, sin, log_g, batch, seq, rows):
    nr = seq // rows
    kq = D_MODEL // RET_DK
    kv = 2 * D_MODEL // RET_DV
    kg = 4 * D_MODEL // RET_DV
    row = lambda b, h, r: b * nr + r
    lg = jnp.broadcast_to(log_g[:, None, None], (RET_HEADS, 1, RET_DV))
    return pl.pallas_call(
        functools.partial(_ret_prompt_kernel, rows=rows),
        grid=(batch, RET_HEADS, nr),
        in_specs=[
            pl.BlockSpec((rows, RET_DK), lambda b, h, r: (row(b, h, r), h)),
            pl.BlockSpec((rows, RET_DK), lambda b, h, r: (row(b, h, r), kq + h)),
            pl.BlockSpec((rows, RET_DV), lambda b, h, r: (row(b, h, r), kv + h)),
            pl.BlockSpec((rows, RET_DV), lambda b, h, r: (row(b, h, r), kg + h)),
            pl.BlockSpec((rows, RET_DK // 2), lambda b, h, r: (r, 0)),
            pl.BlockSpec((rows, RET_DK // 2), lambda b, h, r: (r, 0)),
            pl.BlockSpec((None, 1, RET_DV), lambda b, h, r: (h, 0, 0)),
        ],
        out_specs=[
            pl.BlockSpec((rows, RET_DV), lambda b, h, r: (row(b, h, r), h)),
            pl.BlockSpec((None, None, RET_DK, RET_DV), lambda b, h, r: (b, h, 0, 0)),
        ],
        out_shape=[
            jax.ShapeDtypeStruct((batch * seq, RET_HEADS * RET_DV), BF),
            jax.ShapeDtypeStruct((batch, RET_HEADS, RET_DK, RET_DV), F32),
        ],
        compiler_params=_cparams("parallel", "parallel", "arbitrary"),
        name="retention_prompt",
    )(proj, proj, proj, proj, cos, sin, lg)


def _ret_sample_kernel(p_ref, s0_ref, cos_ref, sin_ref, lg_ref, o_ref, s_ref, *, nb, seq):
    R = nb * seq
    ri = lax.broadcasted_iota(jnp.int32, (R, R), 0)
    rj = lax.broadcasted_iota(jnp.int32, (R, R), 1)
    same = (ri // seq) == (rj // seq)
    diff = (ri - rj).astype(F32)
    t_row = (lax.broadcasted_iota(jnp.int32, (R, 1), 0) % seq).astype(F32)
    b_row = lax.broadcasted_iota(jnp.int32, (R, 1), 0) // seq
    cos, sin = cos_ref[...], sin_ref[...]
    for h in range(RET_HEADS):
        lg = lg_ref[h]
        inner = jnp.where(same & (diff >= 0), jnp.exp(lg[:, :R] * jnp.maximum(diff, 0.0)), 0.0)
        cross = jnp.exp(lg * (t_row + 1.0))
        kdec = jnp.exp(lg[:, :RET_DK] * (seq - 1.0 - t_row))
        cdec = jnp.exp(lg * float(seq))
        q = _rotary(p_ref[:, h * RET_DK:(h + 1) * RET_DK], cos, sin)
        k = _rotary(p_ref[:, D_MODEL + h * RET_DK:D_MODEL + (h + 1) * RET_DK], cos, sin)
        k = k * (RET_DK ** -0.5)
        v = p_ref[:, 2 * D_MODEL + h * RET_DV:2 * D_MODEL + (h + 1) * RET_DV]
        g = p_ref[:, 4 * D_MODEL + h * RET_DV:4 * D_MODEL + (h + 1) * RET_DV]
        qb, vb = q.astype(BF), v.astype(BF)
        sc = lax.dot_general(qb, k.astype(BF), _NT, preferred_element_type=F32) * inner
        o = _dot(sc.astype(BF), vb)
        kd = k * kdec
        for b in range(nb):
            S = s0_ref[b, h]
            mine = b_row == b
            o = o + jnp.where(mine, _dot(qb, S.astype(BF)) * cross, 0.0)
            kb = jnp.where(mine, kd, 0.0).astype(BF)
            s_ref[b, h] = S * cdec + lax.dot_general(kb, vb, _TN, preferred_element_type=F32)
        o = o * lax.rsqrt(jnp.mean(o * o, axis=-1, keepdims=True) + EPS)
        o_ref[:, h * RET_DV:(h + 1) * RET_DV] = o * _silu(g)


def _ret_sample(proj, s0, cos, sin, log_g, seq, nb):
    M = proj.shape[0]
    batch = M // seq
    R = nb * seq
    lg = jnp.broadcast_to(log_g[:, None, None], (RET_HEADS, 1, RET_DV))
    state = pl.BlockSpec((nb, RET_HEADS, RET_DK, RET_DV), lambda i: (i, 0, 0, 0))
    return pl.pallas_call(
        functools.partial(_ret_sample_kernel, nb=nb, seq=seq),
        grid=(batch // nb,),
        in_specs=[
            pl.BlockSpec((R, 6 * D_MODEL), lambda i: (i, 0)),
            state,
            pl.BlockSpec((R, RET_DK // 2), lambda i: (0, 0)),
            pl.BlockSpec((R, RET_DK // 2), lambda i: (0, 0)),
            pl.BlockSpec((RET_HEADS, 1, RET_DV), lambda i: (0, 0, 0)),
        ],
        out_specs=[pl.BlockSpec((R, RET_HEADS * RET_DV), lambda i: (i, 0)), state],
        out_shape=[
            jax.ShapeDtypeStruct((M, RET_HEADS * RET_DV), F32),
            jax.ShapeDtypeStruct(s0.shape, F32),
        ],
        compiler_params=_cparams("parallel"),
        name="retention_sample",
    )(proj, s0, jnp.tile(cos, (nb, 1)), jnp.tile(sin, (nb, 1)), lg)


def _fox_prompt_kernel(qi_ref, kj_ref, q_ref, k_ref, v_ref, fc_ref, fr_ref, o_ref,
                       qm_ref, fcs_ref, m_ref, l_ref, acc_ref):
    tq = q_ref.shape[0]
    tk = k_ref.shape[0]
    p = pl.program_id(2)
    hp = pl.program_id(1)
    qi = qi_ref[p]
    kj = kj_ref[p]
    lane = lax.broadcasted_iota(jnp.int32, (tq, LANES), 1)

    @pl.when(kj == 0)
    def _():
        q = q_ref[...].astype(F32)
        qm_ref[0] = jnp.where(lane < FOX_HD, q, 0.0).astype(BF)
        qm_ref[1] = jnp.where(lane >= FOX_HD, q, 0.0).astype(BF)
        fc = fc_ref[...]
        head = lax.broadcasted_iota(jnp.int32, fc.shape, 1)
        for hh in range(2):
            fcs_ref[hh] = jnp.sum(jnp.where(head == 2 * hp + hh, fc, 0.0), axis=-1, keepdims=True)
        m_ref[...] = jnp.full_like(m_ref, NEG)
        l_ref[...] = jnp.zeros_like(l_ref)
        acc_ref[...] = jnp.zeros_like(acc_ref)

    def step(diagonal):
        k = k_ref[...]
        v = v_ref[...]
        for hh in range(2):
            s = lax.dot_general(qm_ref[hh], k, _NT, preferred_element_type=F32)
            s = s + fcs_ref[hh] - fr_ref[hh:hh + 1, :]
            if diagonal:
                row = lax.broadcasted_iota(jnp.int32, (tq, tk), 0)
                col = lax.broadcasted_iota(jnp.int32, (tq, tk), 1)
                s = jnp.where(row >= col, s, NEG)
            m_old = m_ref[hh]
            m_new = jnp.maximum(m_old, jnp.max(s, axis=-1, keepdims=True))
            alpha = jnp.exp(m_old - m_new)
            pm = jnp.exp(s - m_new)
            l_ref[hh] = alpha * l_ref[hh] + jnp.sum(pm, axis=-1, keepdims=True)
            acc_ref[hh] = alpha * acc_ref[hh] + _dot(pm.astype(BF), v)
            m_ref[hh] = m_new

    @pl.when(kj < qi)
    def _():
        step(False)

    @pl.when(kj == qi)
    def _():
        step(True)
        o0 = acc_ref[0] / l_ref[0]
        o1 = acc_ref[1] / l_ref[1]
        o_ref[...] = jnp.where(lane < FOX_HD, o0, o1).astype(o_ref.dtype)


def _fox_prompt(q, k, v, cum, batch, seq, tq):
    nq = seq // tq
    pairs = [(i, j) for i in range(nq) for j in range(i + 1)]
    qi_tab = jnp.asarray([p[0] for p in pairs], jnp.int32)
    kj_tab = jnp.asarray([p[1] for p in pairs], jnp.int32)
    hp_n = FOX_HEADS // 2
    f_col = cum.reshape(batch, seq, FOX_HEADS)
    f_row = cum.reshape(batch, seq, hp_n, 2).transpose(0, 2, 3, 1)
    grid_spec = pltpu.PrefetchScalarGridSpec(
        num_scalar_prefetch=2,
        grid=(batch, hp_n, len(pairs)),
        in_specs=[
            pl.BlockSpec((tq, LANES), lambda b, h, p, qi, kj: (b * nq + qi[p], h)),
            pl.BlockSpec((tq, LANES), lambda b, h, p, qi, kj: (b * nq + kj[p], h)),
            pl.BlockSpec((tq, LANES), lambda b, h, p, qi, kj: (b * nq + kj[p], h)),
            pl.BlockSpec((None, tq, FOX_HEADS), lambda b, h, p, qi, kj: (b, qi[p], 0)),
            pl.BlockSpec((None, None, 2, tq), lambda b, h, p, qi, kj: (b, h, 0, kj[p])),
        ],
        out_specs=pl.BlockSpec((tq, LANES), lambda b, h, p, qi, kj: (b * nq + qi[p], h)),
        scratch_shapes=[
            pltpu.VMEM((2, tq, LANES), BF),
            pltpu.VMEM((2, tq, 1), F32),
            pltpu.VMEM((2, tq, 1), F32),
            pltpu.VMEM((2, tq, 1), F32),
            pltpu.VMEM((2, tq, LANES), F32),
        ],
    )
    return pl.pallas_call(
        _fox_prompt_kernel,
        grid_spec=grid_spec,
        out_shape=jax.ShapeDtypeStruct((batch * seq, D_MODEL), BF),
        compiler_params=_cparams("parallel", "parallel", "arbitrary"),
        name="fox_prompt_attention",
    )(qi_tab, kj_tab, q, k, v, f_col, f_row)


def _fox_sample_kernel(pt_ref, q_ref, kn_ref, vn_ref, lfn_ref, *refs, npg, seq):
    k_refs = refs[:npg]
    v_refs = refs[npg:2 * npg]
    lf_refs = refs[2 * npg:3 * npg]
    o_ref = refs[3 * npg]
    qbd_ref, carry_ref, fn_ref, m_ref, l_ref, acc_ref, kpad_ref, vpad_ref = refs[3 * npg + 1:]
    g = pl.program_id(1)
    C = seq * FOX_HEADS
    P = k_refs[0].shape[0]
    row = lax.broadcasted_iota(jnp.int32, (C, D_MODEL), 0)
    col = lax.broadcasted_iota(jnp.int32, (C, D_MODEL), 1)
    own_head = (col >> 6) == (row % FOX_HEADS)
    t_of = lax.broadcasted_iota(jnp.int32, (C, P), 0) // FOX_HEADS
    j_of = lax.broadcasted_iota(jnp.int32, (C, P), 1)
    kk = lax.broadcasted_iota(jnp.int32, (P, P), 0)
    jj = lax.broadcasted_iota(jnp.int32, (P, P), 1)

    def tile_heads(a):
        return jnp.concatenate([a] * seq, axis=0)

    @pl.when(g == 0)
    def _():
        q = q_ref[...]
        qrep = jnp.concatenate(
            [jnp.broadcast_to(q[t:t + 1, :], (FOX_HEADS, D_MODEL)) for t in range(seq)], axis=0)
        qbd = jnp.where(own_head, qrep, 0.0).astype(BF)
        qbd_ref[...] = qbd
        kpad_ref[...] = jnp.zeros_like(kpad_ref)
        vpad_ref[...] = jnp.zeros_like(vpad_ref)
        kpad_ref[0:seq, :] = kn_ref[...]
        vpad_ref[0:seq, :] = vn_ref[...]
        fn = _dot_exact(lfn_ref[...], jnp.where(kk <= jj, 1.0, 0.0))
        fn_c = tile_heads(fn)
        fn_col = jnp.sum(jnp.where(j_of == t_of, fn_c, 0.0), axis=-1, keepdims=True)
        fn_ref[...] = fn_col
        s = lax.dot_general(qbd, kpad_ref[...].astype(BF), _NT, preferred_element_type=F32)
        s = jnp.where(j_of <= t_of, s + fn_col - fn_c, NEG)
        m = jnp.max(s, axis=-1, keepdims=True)
        pm = jnp.exp(s - m)
        m_ref[...] = m
        l_ref[...] = jnp.sum(pm, axis=-1, keepdims=True)
        acc_ref[...] = _dot(pm.astype(BF), vpad_ref[...].astype(BF))
        carry_ref[...] = jnp.zeros_like(carry_ref)

    later = jnp.where(kk > jj, 1.0, 0.0)
    qbd = qbd_ref[...]
    fn_col = fn_ref[...]
    carry = carry_ref[...]
    scores = []
    for i in reversed(range(npg)):
        lf = lf_refs[i][...]
        r_page = _dot_exact(lf, later) + carry
        carry = carry + jnp.sum(lf, axis=-1, keepdims=True)
        s = lax.dot_general(qbd, k_refs[i][...].astype(BF), _NT, preferred_element_type=F32)
        scores.append((i, s + fn_col + tile_heads(r_page)))
    carry_ref[...] = carry
    m_old = m_ref[...]
    m_new = m_old
    for _, s in scores:
        m_new = jnp.maximum(m_new, jnp.max(s, axis=-1, keepdims=True))
    alpha = jnp.exp(m_old - m_new)
    l_new = alpha * l_ref[...]
    acc = alpha * acc_ref[...]
    for i, s in scores:
        pm = jnp.exp(s - m_new)
        l_new = l_new + jnp.sum(pm, axis=-1, keepdims=True)
        acc = acc + _dot(pm.astype(BF), v_refs[i][...].astype(BF))
    m_ref[...] = m_new
    l_ref[...] = l_new
    acc_ref[...] = acc

    @pl.when(g == pl.num_programs(1) - 1)
    def _():
        on = jnp.where(own_head, acc / l_new, 0.0)
        for t in range(seq):
            o_ref[t:t + 1, :] = jnp.sum(on[t * FOX_HEADS:(t + 1) * FOX_HEADS, :], axis=0, keepdims=True)


def _fox_sample(q, k_new, v_new, lf_new_t, cache_k, cache_v, cache_lf_t, page_table, seq, npg):
    batch, n_pages = page_table.shape
    P = cache_k.shape[1]
    groups = n_pages // npg
    C = seq * FOX_HEADS

    def page(i):
        return lambda b, g, pt: (pt[b, (groups - 1 - g) * npg + i], 0, 0)

    new_tok = pl.BlockSpec((None, seq, D_MODEL), lambda b, g, pt: (b, 0, 0))
    grid_spec = pltpu.PrefetchScalarGridSpec(
        num_scalar_prefetch=1,
        grid=(batch, groups),
        in_specs=(
            [new_tok, new_tok, new_tok,
             pl.BlockSpec((None, FOX_HEADS, P), lambda b, g, pt: (b, 0, 0))]
            + [pl.BlockSpec((None, P, D_MODEL), page(i)) for i in range(npg)]
            + [pl.BlockSpec((None, P, D_MODEL), page(i)) for i in range(npg)]
            + [pl.BlockSpec((None, FOX_HEADS, P), page(i)) for i in range(npg)]
        ),
        out_specs=new_tok,
        scratch_shapes=[
            pltpu.VMEM((C, D_MODEL), BF),
            pltpu.VMEM((FOX_HEADS, 1), F32),
            pltpu.VMEM((C, 1), F32),
            pltpu.VMEM((C, 1), F32),
            pltpu.VMEM((C, 1), F32),
            pltpu.VMEM((C, D_MODEL), F32),
            pltpu.VMEM((P, D_MODEL), F32),
            pltpu.VMEM((P, D_MODEL), F32),
        ],
    )
    return pl.pallas_call(
        functools.partial(_fox_sample_kernel, npg=npg, seq=seq),
        grid_spec=grid_spec,
        out_shape=jax.ShapeDtypeStruct((batch, seq, D_MODEL), F32),
        compiler_params=_cparams("parallel", "arbitrary"),
        name="fox_sample_attention",
    )(page_table, q, k_new, v_new, lf_new_t,
      *([cache_k] * npg), *([cache_v] * npg), *([cache_lf_t] * npg))


def _prep_weights(w_ret_in, w_ret_out, w_q, w_o, w_ffn_in, w_ffn_out, w_kvf):
    depth, _, two_f = w_ffn_in.shape
    d_ff = two_f // 2
    nf = d_ff // FFN_CHUNK
    wi = w_ffn_in.astype(BF).reshape(depth, D_MODEL, 2 * nf, FFN_CHUNK).transpose(0, 2, 1, 3)
    wo = w_ffn_out.astype(BF).reshape(depth, nf, FFN_CHUNK, D_MODEL)
    w_kvf_b = w_kvf.astype(BF)
    wf = jnp.zeros((D_MODEL, LANES), BF).at[:, :FOX_HEADS].set(w_kvf_b[:, 2 * D_MODEL:])
    return dict(
        ret_in=w_ret_in.astype(BF), ret_out=w_ret_out.astype(BF), q=w_q.astype(BF), o=w_o.astype(BF),
        ffn_in=wi, ffn_out=wo, wk=w_kvf_b[:, :D_MODEL], wv=w_kvf_b[:, D_MODEL:2 * D_MODEL], wf=wf)


def _rope_tables(pos):
    half = RET_DK // 2
    freqs = ROPE_BASE ** (-jnp.arange(half, dtype=F32) / half)
    ang = pos.astype(F32)[:, None] * freqs[None, :]
    return jnp.cos(ang), jnp.sin(ang)


def _trunk(x, mods, kvmod, mod, pos, W, norms, b_f, *, batch, seq, tm, ret_fn, attn_fn):
    ret_norm_g, attn_norm_g, q_norm_g, ffn_norm_g, kv_norm_g, k_norm_g = norms
    n_a = W["ret_in"].shape[0]
    depth = W["ffn_in"].shape[0]
    log_g = jnp.log1p(-jnp.exp2(-5.0 - jnp.arange(RET_HEADS, dtype=F32)))
    cos, sin = _rope_tables(pos)
    states = []
    shared = None
    for l in range(depth):
        sh1, sc1, g1, sh2, sc2, g2 = mods[l]
        if l < n_a:
            a, S = ret_fn(l, x, ret_norm_g[l], sh1, sc1, W["ret_in"][l], cos, sin, log_g)
            states.append(S)
            x = _out_proj(a, W["ret_out"][l], x, g1, mod, tm)
        else:
            j = l - n_a
            if shared is None:
                shared = _kvf_proj(x, kv_norm_g, kvmod[0], kvmod[1], W["wk"], W["wv"], W["wf"],
                                   b_f, k_norm_g, mod, min(tm, 512), seq)
            a = attn_fn(j, x, attn_norm_g[j], sh1, sc1, W["q"][j], q_norm_g[j], shared)
            x = _out_proj(a, W["o"][j], x, g1, mod, tm)
        x = _ffn(x, ffn_norm_g[l], sh2, sc2, g2, W["ffn_in"][l], W["ffn_out"][l], mod, tm)
    return x, jnp.stack(states), shared


def kernel(x_prompt, x_sample, cache_k, cache_v, cache_logf, state_ret, page_table, c_prompt, c_sample,
           w_mod, b_mod, ret_norm_g, w_ret_in, w_ret_out, attn_norm_g, w_q, q_norm_g, w_o, ffn_norm_g,
           w_ffn_in, w_ffn_out, kv_norm_g, w_kvmod, b_kvmod, w_kvf, b_f, k_norm_g):
    Bp, Tp, _ = x_prompt.shape
    Bd, Td, _ = x_sample.shape
    depth = w_mod.shape[0]
    n_pool, page_size = cache_k.shape[:2]
    past_len = page_table.shape[1] * page_size
    W = _prep_weights(w_ret_in, w_ret_out, w_q, w_o, w_ffn_in, w_ffn_out, w_kvf)
    norms = (ret_norm_g, attn_norm_g, q_norm_g, ffn_norm_g, kv_norm_g, k_norm_g)

    c_all = jnp.concatenate([c_prompt, c_sample], axis=0)
    mod_all = _cond_proj(c_all, w_mod, b_mod, 1536)
    kvmod_all = _cond_proj(c_all, w_kvmod[None], b_kvmod[None], 1024)[0]

    def split_mods(lo, hi, expand):
        def lay(a):
            return jnp.repeat(a, expand, axis=0)[None] if expand else a[:, None, :]
        mods = [[lay(m) for m in jnp.split(mod_all[l, lo:hi], 6, axis=-1)] for l in range(depth)]
        kvmod = [lay(m) for m in jnp.split(kvmod_all[lo:hi], 2, axis=-1)]
        return mods, kvmod

    mods_p, kvmod_p = split_mods(0, Bp, 0)
    mod_p = _Mod(Tp, per_row=False)
    tm_p = 1024

    def ret_prompt(l, x, g, sh, sc, w, cos, sin, log_g):
        proj = _nm_matmul(x, g, sh, sc, w, mod_p, tm_p, 1536, BF)
        return _ret_prompt(proj, cos, sin, log_g, Bp, Tp, 512)

    def attn_prompt(j, x, g, sh, sc, w, gq, shared):
        _, _, _, cum, kb, vb = shared
        q = _q_proj(x, g, sh, sc, w, gq, mod_p, tm_p, BF)
        return _fox_prompt(q, kb, vb, cum, Bp, Tp, 512)

    y_p, st_p, (k_p, v_p, lf_p, _, _, _) = _trunk(
        x_prompt.reshape(Bp * Tp, D_MODEL), mods_p, kvmod_p, mod_p, jnp.arange(Tp), W, norms, b_f,
        batch=Bp, seq=Tp, tm=tm_p, ret_fn=ret_prompt, attn_fn=attn_prompt)

    mods_s, kvmod_s = split_mods(Bp, Bp + Bd, Td)
    Ms = Bd * Td
    mod_s = _Mod(Ms, per_row=True)
    ck = cache_k.reshape(n_pool, page_size, D_MODEL)
    cv = cache_v.reshape(n_pool, page_size, D_MODEL)
    clf_t = cache_logf.transpose(0, 2, 1)

    def ret_sample(l, x, g, sh, sc, w, cos, sin, log_g):
        proj = _nm_matmul(x, g, sh, sc, w, mod_s, Ms, 1536, F32)
        return _ret_sample(proj, state_ret[l], cos, sin, log_g, Td, 2)

    def attn_sample(j, x, g, sh, sc, w, gq, shared):
        k, v, lf = shared[:3]
        q = _q_proj(x, g, sh, sc, w, gq, mod_s, Ms, F32)
        lf_t = jnp.pad(lf.reshape(Bd, Td, FOX_HEADS).transpose(0, 2, 1),
                       ((0, 0), (0, 0), (0, page_size - Td)))
        o = _fox_sample(q.reshape(Bd, Td, D_MODEL), k.reshape(Bd, Td, D_MODEL), v.reshape(Bd, Td, D_MODEL),
                        lf_t, ck, cv, clf_t, page_table, Td, 8)
        return o.reshape(Ms, D_MODEL)

    y_s, st_s, (k_s, v_s, lf_s, _, _, _) = _trunk(
        x_sample.reshape(Ms, D_MODEL), mods_s, kvmod_s, mod_s, past_len + jnp.arange(Td), W, norms, b_f,
        batch=Bd, seq=Td, tm=Ms, ret_fn=ret_sample, attn_fn=attn_sample)

    heads = (FOX_HEADS, FOX_HD)
    return (y_p.reshape(Bp, Tp, D_MODEL), y_s.reshape(Bd, Td, D_MODEL), st_p, st_s,
            k_p.reshape(Bp, Tp, *heads), v_p.reshape(Bp, Tp, *heads), lf_p.reshape(Bp, Tp, FOX_HEADS),
            k_s.reshape(Bd, Td, *heads), v_s.reshape(Bd, Td, *heads), lf_s.reshape(Bd, Td, FOX_HEADS))
```

```python
import functools
import math

import jax
import jax.numpy as jnp
from jax import lax
from jax.experimental import pallas as pl
from jax.experimental.pallas import tpu as pltpu

BF = jnp.bfloat16
F32 = jnp.float32

D_MODEL = 1024
RET_HEADS = 4
RET_DK = 256
RET_DV = 512
RET_CHUNK = 128
ROPE_BASE = 10000.0
FOX_HEADS = 16
FOX_HD = 64
EPS = 1e-6
LANES = 128
VMEM_LIMIT = 52 * 1024 * 1024
NEG = -1e30

_NT = (((1,), (1,)), ((), ()))
_TN = (((0,), (0,)), ((), ()))


def _cparams(*sem):
    return pltpu.CompilerParams(dimension_semantics=sem, vmem_limit_bytes=VMEM_LIMIT)


def _dot(a, b):
    return jnp.dot(a, b, preferred_element_type=F32)


def _dot_exact(a, b):
    return jnp.dot(a, b, preferred_element_type=F32, precision=lax.Precision.HIGHEST)


def _silu(x):
    return x * jax.nn.sigmoid(x)


def _normmod(x, g, shift, scale):
    ms = jnp.mean(x * x, axis=-1, keepdims=True)
    y = x * lax.rsqrt(ms + EPS) * g
    return y * (1.0 + scale) + shift


def _split_bf16(x):
    hi = x.astype(BF)
    lo = (x - hi.astype(F32)).astype(BF)
    return hi, lo


def _headnorm(a):
    r = lax.broadcasted_iota(jnp.int32, (D_MODEL, LANES), 0) >> 6
    c = lax.broadcasted_iota(jnp.int32, (D_MODEL, LANES), 1)
    seg = jnp.where(r == c, 1.0, 0.0).astype(BF)
    rt = lax.broadcasted_iota(jnp.int32, (LANES, D_MODEL), 0)
    ct = lax.broadcasted_iota(jnp.int32, (LANES, D_MODEL), 1) >> 6
    seg_t = jnp.where(rt == ct, 1.0, 0.0).astype(BF)
    y_hi, y_lo = _split_bf16(a * a)
    ms = (_dot(y_hi, seg) + _dot(y_lo, seg)) * (1.0 / FOX_HD)
    rs = lax.rsqrt(ms + EPS)
    r_hi, r_lo = _split_bf16(rs)
    return a * (_dot(r_hi, seg_t) + _dot(r_lo, seg_t))


def _log_sigmoid(x):
    return jnp.minimum(x, 0.0) - jnp.log1p(jnp.exp(-jnp.abs(x)))


def _cond_kernel(c_ref, w_ref, b_ref, o_ref):
    ca = _silu(c_ref[...]).astype(BF)
    o_ref[...] = _dot(ca, w_ref[...].astype(BF)) + b_ref[...]


def _cond_proj(c, w, b, tn):
    L, K, N = w.shape
    R = c.shape[0]
    return pl.pallas_call(
        _cond_kernel,
        grid=(L, N // tn),
        in_specs=[
            pl.BlockSpec((R, K), lambda l, j: (0, 0)),
            pl.BlockSpec((None, K, tn), lambda l, j: (l, 0, j)),
            pl.BlockSpec((None, 1, tn), lambda l, j: (l, 0, j)),
        ],
        out_specs=pl.BlockSpec((None, R, tn), lambda l, j: (l, 0, j)),
        out_shape=jax.ShapeDtypeStruct((L, R, N), F32),
        compiler_params=_cparams("parallel", "parallel"),
        name="cond_proj",
    )(c, w, b.reshape(L, 1, N))


class _Mod:
    def __init__(self, rows_per_group, per_row):
        self.rows_per_group = rows_per_group
        self.per_row = per_row

    def spec(self, tm):
        if self.per_row:
            return pl.BlockSpec((None, tm, D_MODEL), lambda i, *_: (0, i, 0))
        tiles = self.rows_per_group // tm
        return pl.BlockSpec((None, 1, D_MODEL), lambda i, *_: (i // tiles, 0, 0))


def _vec_spec(n):
    return pl.BlockSpec((1, n), lambda i, *_: (0, 0))


def _nm_kernel(x_ref, g_ref, sh_ref, sc_ref, w_ref, o_ref, h_ref):
    @pl.when(pl.program_id(1) == 0)
    def _():
        h_ref[...] = _normmod(x_ref[...], g_ref[...], sh_ref[...], sc_ref[...]).astype(BF)

    o_ref[...] = _dot(h_ref[...], w_ref[...]).astype(o_ref.dtype)


def _nm_matmul(x, g, shift, scale, w, mod, tm, tn, out_dtype):
    M, K = x.shape
    N = w.shape[1]
    return pl.pallas_call(
        _nm_kernel,
        grid=(M // tm, N // tn),
        in_specs=[
            pl.BlockSpec((tm, K), lambda i, j: (i, 0)),
            _vec_spec(K),
            mod.spec(tm),
            mod.spec(tm),
            pl.BlockSpec((K, tn), lambda i, j: (0, j)),
        ],
        out_specs=pl.BlockSpec((tm, tn), lambda i, j: (i, j)),
        out_shape=jax.ShapeDtypeStruct((M, N), out_dtype),
        scratch_shapes=[pltpu.VMEM((tm, K), BF)],
        compiler_params=_cparams("parallel", "arbitrary"),
        name="norm_mod_matmul",
    )(x, g.reshape(1, K), shift, scale, w)


def _q_kernel(x_ref, g_ref, sh_ref, sc_ref, w_ref, gq_ref, o_ref):
    h = _normmod(x_ref[...], g_ref[...], sh_ref[...], sc_ref[...]).astype(BF)
    q = _headnorm(_dot(h, w_ref[...])) * gq_ref[...]
    o_ref[...] = (q * (FOX_HD ** -0.5)).astype(o_ref.dtype)


def _q_proj(x, g, shift, scale, w, gq, mod, tm, out_dtype):
    M, K = x.shape
    return pl.pallas_call(
        _q_kernel,
        grid=(M // tm,),
        in_specs=[
            pl.BlockSpec((tm, K), lambda i: (i, 0)),
            _vec_spec(K),
            mod.spec(tm),
            mod.spec(tm),
            pl.BlockSpec((K, D_MODEL), lambda i: (0, 0)),
            _vec_spec(D_MODEL),
        ],
        out_specs=pl.BlockSpec((tm, D_MODEL), lambda i: (i, 0)),
        out_shape=jax.ShapeDtypeStruct((M, D_MODEL), out_dtype),
        compiler_params=_cparams("parallel"),
        name="q_proj_headnorm",
    )(x, g.reshape(1, K), shift, scale, w, jnp.tile(gq, FOX_HEADS).reshape(1, D_MODEL))


def _kvf_kernel(x_ref, g_ref, sh_ref, sc_ref, wk_ref, wv_ref, wf_ref, bf_ref, gk_ref,
                k_ref, v_ref, lf_ref, cum_ref, kb_ref, vb_ref, carry_ref, *, tiles_per_seq):
    tm = x_ref.shape[0]
    h = _normmod(x_ref[...], g_ref[...], sh_ref[...], sc_ref[...]).astype(BF)
    k = _headnorm(_dot(h, wk_ref[...])) * gk_ref[...]
    v = _dot(h, wv_ref[...])
    k_ref[...] = k
    v_ref[...] = v
    kb_ref[...] = k.astype(BF)
    vb_ref[...] = v.astype(BF)
    lf = _log_sigmoid(_dot(h, wf_ref[...]) + bf_ref[...])
    lf_ref[...] = lf[:, :FOX_HEADS]

    @pl.when(pl.program_id(0) % tiles_per_seq == 0)
    def _():
        carry_ref[...] = jnp.zeros_like(carry_ref)

    r = lax.broadcasted_iota(jnp.int32, (tm, tm), 0)
    c = lax.broadcasted_iota(jnp.int32, (tm, tm), 1)
    tri = jnp.where(r >= c, 1.0, 0.0)
    cum = _dot_exact(tri, lf) + carry_ref[...]
    carry_ref[...] = cum[tm - 1:tm, :]
    cum_ref[...] = cum[:, :FOX_HEADS]


def _kvf_proj(x, g, shift, scale, wk, wv, wf, b_f, gk, mod, tm, seq_len):
    M, K = x.shape
    tok = pl.BlockSpec((tm, D_MODEL), lambda i: (i, 0))
    hd = pl.BlockSpec((tm, FOX_HEADS), lambda i: (i, 0))
    w_spec = pl.BlockSpec((K, D_MODEL), lambda i: (0, 0))
    bf_pad = jnp.zeros((1, LANES), F32).at[0, :FOX_HEADS].set(b_f)
    return pl.pallas_call(
        functools.partial(_kvf_kernel, tiles_per_seq=max(seq_len // tm, 1)),
        grid=(M // tm,),
        in_specs=[
            pl.BlockSpec((tm, K), lambda i: (i, 0)),
            _vec_spec(K),
            mod.spec(tm),
            mod.spec(tm),
            w_spec,
            w_spec,
            pl.BlockSpec((K, LANES), lambda i: (0, 0)),
            _vec_spec(LANES),
            _vec_spec(D_MODEL),
        ],
        out_specs=[tok, tok, hd, hd, tok, tok],
        out_shape=[
            jax.ShapeDtypeStruct((M, D_MODEL), F32),
            jax.ShapeDtypeStruct((M, D_MODEL), F32),
            jax.ShapeDtypeStruct((M, FOX_HEADS), F32),
            jax.ShapeDtypeStruct((M, FOX_HEADS), F32),
            jax.ShapeDtypeStruct((M, D_MODEL), BF),
            jax.ShapeDtypeStruct((M, D_MODEL), BF),
        ],
        scratch_shapes=[pltpu.VMEM((1, LANES), F32)],
        compiler_params=_cparams("arbitrary"),
        name="shared_kvf",
    )(x, g.reshape(1, K), shift, scale, wk, wv, wf, bf_pad,
      jnp.tile(gk, FOX_HEADS).reshape(1, D_MODEL))


def _out_kernel(a_ref, w_ref, x_ref, gt_ref, o_ref):
    y = _dot(a_ref[...].astype(BF), w_ref[...])
    o_ref[...] = x_ref[...] + (1.0 + gt_ref[...]) * y


def _out_proj(a, w, x, gate, mod, tm):
    M, K = a.shape
    return pl.pallas_call(
        _out_kernel,
        grid=(M // tm,),
        in_specs=[
            pl.BlockSpec((tm, K), lambda i: (i, 0)),
            pl.BlockSpec((K, D_MODEL), lambda i: (0, 0)),
            pl.BlockSpec((tm, D_MODEL), lambda i: (i, 0)),
            mod.spec(tm),
        ],
        out_specs=pl.BlockSpec((tm, D_MODEL), lambda i: (i, 0)),
        out_shape=jax.ShapeDtypeStruct((M, D_MODEL), F32),
        compiler_params=_cparams("parallel"),
        name="out_proj_residual",
    )(a, w, x, gate)


FFN_CHUNK = 256


def _ffn_kernel(x_ref, g_ref, sh_ref, sc_ref, gt_ref, wi_ref, wo_ref, o_ref, h_ref, acc_ref, *, nf):
    h_ref[...] = _normmod(x_ref[...], g_ref[...], sh_ref[...], sc_ref[...]).astype(BF)
    acc_ref[...] = jnp.zeros_like(acc_ref)

    def body(f, carry):
        h = h_ref[...]
        gate = _dot(h, wi_ref[f])
        up = _dot(h, wi_ref[nf + f])
        act = (_silu(gate) * up).astype(BF)
        acc_ref[...] += _dot(act, wo_ref[f])
        return carry

    lax.fori_loop(0, nf, body, 0)
    o_ref[...] = x_ref[...] + (1.0 + gt_ref[...]) * acc_ref[...]


def _ffn(x, g, shift, scale, gate, wi, wo, mod, tm):
    M, K = x.shape
    nf = wo.shape[0]
    resident = pl.Buffered(1)
    return pl.pallas_call(
        functools.partial(_ffn_kernel, nf=nf),
        grid=(M // tm,),
        in_specs=[
            pl.BlockSpec((tm, K), lambda i: (i, 0)),
            _vec_spec(K),
            mod.spec(tm),
            mod.spec(tm),
            mod.spec(tm),
            pl.BlockSpec(wi.shape, lambda i: (0, 0, 0), pipeline_mode=resident),
            pl.BlockSpec(wo.shape, lambda i: (0, 0, 0), pipeline_mode=resident),
        ],
        out_specs=pl.BlockSpec((tm, K), lambda i: (i, 0)),
        out_shape=jax.ShapeDtypeStruct((M, K), F32),
        scratch_shapes=[pltpu.VMEM((tm, K), BF), pltpu.VMEM((tm, K), F32)],
        compiler_params=_cparams("parallel"),
        name="ffn_swiglu",
    )(x, g.reshape(1, K), shift, scale, gate, wi, wo)


def _rotary(x, cos, sin):
    half = x.shape[-1] // 2
    x1, x2 = x[:, :half], x[:, half:]
    return jnp.concatenate([x1 * cos - x2 * sin, x1 * sin + x2 * cos], axis=-1)


def _ret_prompt_kernel(q_ref, k_ref, v_ref, g_ref, cos_ref, sin_ref, lg_ref, o_ref, s_ref, *, rows):
    C = RET_CHUNK

    @pl.when(pl.program_id(2) == 0)
    def _():
        s_ref[...] = jnp.zeros_like(s_ref)

    lg = lg_ref[...]
    i_v = lax.broadcasted_iota(jnp.int32, (C, RET_DV), 0).astype(F32)
    i_k = lax.broadcasted_iota(jnp.int32, (C, RET_DK), 0).astype(F32)
    cross = jnp.exp(lg * (i_v + 1.0))
    kdec = jnp.exp(lg_ref[:, :RET_DK] * (C - 1.0 - i_k))
    cdec = jnp.exp(lg * float(C))
    ii = lax.broadcasted_iota(jnp.int32, (C, C), 0)
    jj = lax.broadcasted_iota(jnp.int32, (C, C), 1)
    diff = (ii - jj).astype(F32)
    inner = jnp.where(diff >= 0, jnp.exp(lg_ref[:, :C] * jnp.maximum(diff, 0.0)), 0.0)

    for c in range(rows // C):
        sl = slice(c * C, (c + 1) * C)
        cos, sin = cos_ref[sl, :], sin_ref[sl, :]
        q = _rotary(q_ref[sl, :].astype(F32), cos, sin)
        k = _rotary(k_ref[sl, :].astype(F32), cos, sin) * (RET_DK ** -0.5)
        v = v_ref[sl, :]
        qb = q.astype(BF)
        S = s_ref[...]
        sc = lax.dot_general(qb, k.astype(BF), _NT, preferred_element_type=F32) * inner
        o = _dot(sc.astype(BF), v) + _dot(qb, S.astype(BF)) * cross
        s_ref[...] = S * cdec + lax.dot_general((k * kdec).astype(BF), v, _TN,
                                                preferred_element_type=F32)
        o = o * lax.rsqrt(jnp.mean(o * o, axis=-1, keepdims=True) + EPS)
        o_ref[sl, :] = (o * _silu(g_ref[sl, :].astype(F32))).astype(o_ref.dtype)


def _ret_prompt(proj, cos, sin, log_g, batch, seq, rows):
    nr = seq // rows
    kq = D_MODEL // RET_DK
    kv = 2 * D_MODEL // RET_DV
    kg = 4 * D_MODEL // RET_DV
    row = lambda b, h, r: b * nr + r
    lg = jnp.broadcast_to(log_g[:, None, None], (RET_HEADS, 1, RET_DV))
    return pl.pallas_call(
        functools.partial(_ret_prompt_kernel, rows=rows),
        grid=(batch, RET_HEADS, nr),
        in_specs=[
            pl.BlockSpec((rows, RET_DK), lambda b, h, r: (row(b, h, r), h)),
            pl.BlockSpec((rows, RET_DK), lambda b, h, r: (row(b, h, r), kq + h)),
            pl.BlockSpec((rows, RET_DV), lambda b, h, r: (row(b, h, r), kv + h)),
            pl.BlockSpec((rows, RET_DV), lambda b, h, r: (row(b, h, r), kg + h)),
            pl.BlockSpec((rows, RET_DK // 2), lambda b, h, r: (r, 0)),
            pl.BlockSpec((rows, RET_DK // 2), lambda b, h, r: (r, 0)),
            pl.BlockSpec((None, 1, RET_DV), lambda b, h, r: (h, 0, 0)),
        ],
        out_specs=[
            pl.BlockSpec((rows, RET_DV), lambda b, h, r: (row(b, h, r), h)),
            pl.BlockSpec((None, None, RET_DK, RET_DV), lambda b, h, r: (b, h, 0, 0)),
        ],
        out_shape=[
            jax.ShapeDtypeStruct((batch * seq, RET_HEADS * RET_DV), BF),
            jax.ShapeDtypeStruct((batch, RET_HEADS, RET_DK, RET_DV), F32),
        ],
        compiler_params=_cparams("parallel", "parallel", "arbitrary"),
        name="retention_prompt",
    )(proj, proj, proj, proj, cos, sin, lg)


def _ret_sample_kernel(p_ref, s0_ref, cos_ref, sin_ref, lg_ref, o_ref, s_ref, *, nb, seq):
    R = nb * seq
    ri = lax.broadcasted_iota(jnp.int32, (R, R), 0)
    rj = lax.broadcasted_iota(jnp.int32, (R, R), 1)
    same = (ri // seq) == (rj // seq)
    diff = (ri - rj).astype(F32)
    t_row = (lax.broadcasted_iota(jnp.int32, (R, 1), 0) % seq).astype(F32)
    b_row = lax.broadcasted_iota(jnp.int32, (R, 1), 0) // seq
    cos, sin = cos_ref[...], sin_ref[...]
    for h in range(RET_HEADS):
        lg = lg_ref[h]
        inner = jnp.where(same & (diff >= 0), jnp.exp(lg[:, :R] * jnp.maximum(diff, 0.0)), 0.0)
        cross = jnp.exp(lg * (t_row + 1.0))
        kdec = jnp.exp(lg[:, :RET_DK] * (seq - 1.0 - t_row))
        cdec = jnp.exp(lg * float(seq))
        q = _rotary(p_ref[:, h * RET_DK:(h + 1) * RET_DK], cos, sin)
        k = _rotary(p_ref[:, D_MODEL + h * RET_DK:D_MODEL + (h + 1) * RET_DK], cos, sin)
        k = k * (RET_DK ** -0.5)
        v = p_ref[:, 2 * D_MODEL + h * RET_DV:2 * D_MODEL + (h + 1) * RET_DV]
        g = p_ref[:, 4 * D_MODEL + h * RET_DV:4 * D_MODEL + (h + 1) * RET_DV]
        qb, vb = q.astype(BF), v.astype(BF)
        sc = lax.dot_general(qb, k.astype(BF), _NT, preferred_element_type=F32) * inner
        o = _dot(sc.astype(BF), vb)
        kd = k * kdec
        for b in range(nb):
            S = s0_ref[b, h]
            mine = b_row == b
            o = o + jnp.where(mine, _dot(qb, S.astype(BF)) * cross, 0.0)
            kb = jnp.where(mine, kd, 0.0).astype(BF)
            s_ref[b, h] = S * cdec + lax.dot_general(kb, vb, _TN, preferred_element_type=F32)
        o = o * lax.rsqrt(jnp.mean(o * o, axis=-1, keepdims=True) + EPS)
        o_ref[:, h * RET_DV:(h + 1) * RET_DV] = o * _silu(g)


def _ret_sample(proj, s0, cos, sin, log_g, seq, nb):
    M = proj.shape[0]
    batch = M // seq
    R = nb * seq
    lg = jnp.broadcast_to(log_g[:, None, None], (RET_HEADS, 1, RET_DV))
    state = pl.BlockSpec((nb, RET_HEADS, RET_DK, RET_DV), lambda i: (i, 0, 0, 0))
    return pl.pallas_call(
        functools.partial(_ret_sample_kernel, nb=nb, seq=seq),
        grid=(batch // nb,),
        in_specs=[
            pl.BlockSpec((R, 6 * D_MODEL), lambda i: (i, 0)),
            state,
            pl.BlockSpec((R, RET_DK // 2), lambda i: (0, 0)),
            pl.BlockSpec((R, RET_DK // 2), lambda i: (0, 0)),
            pl.BlockSpec((RET_HEADS, 1, RET_DV), lambda i: (0, 0, 0)),
        ],
        out_specs=[pl.BlockSpec((R, RET_HEADS * RET_DV), lambda i: (i, 0)), state],
        out_shape=[
            jax.ShapeDtypeStruct((M, RET_HEADS * RET_DV), F32),
            jax.ShapeDtypeStruct(s0.shape, F32),
        ],
        compiler_params=_cparams("parallel"),
        name="retention_sample",
    )(proj, s0, jnp.tile(cos, (nb, 1)), jnp.tile(sin, (nb, 1)), lg)


def _fox_prompt_kernel(qi_ref, kj_ref, q_ref, k_ref, vt_ref, fq_ref, fk_ref, o_ref,
                       qm_ref, m_ref, l_ref, acc_ref):
    tq = q_ref.shape[0]
    tk = k_ref.shape[0]
    p = pl.program_id(2)
    hp = pl.program_id(1)
    qi = qi_ref[p]
    kj = kj_ref[p]

    @pl.when(kj == 0)
    def _():
        lane = lax.broadcasted_iota(jnp.int32, (tq, LANES), 1)
        q = q_ref[...].astype(F32)
        qm_ref[0] = jnp.where(lane < FOX_HD, q, 0.0).astype(BF)
        qm_ref[1] = jnp.where(lane >= FOX_HD, q, 0.0).astype(BF)
        m_ref[...] = jnp.full_like(m_ref, NEG)
        l_ref[...] = jnp.zeros_like(l_ref)
        acc_ref[...] = jnp.zeros_like(acc_ref)

    def step(diagonal):
        k = k_ref[...]
        vt = vt_ref[...]
        fk = fk_ref[...]
        head = lax.broadcasted_iota(jnp.int32, fk.shape, 1)
        for hh in range(2):
            fk_col = jnp.sum(jnp.where(head == 2 * hp + hh, fk, 0.0), axis=1, keepdims=True)
            st = lax.dot_general(k, qm_ref[hh], _NT, preferred_element_type=F32) - fk_col
            if diagonal:
                krow = lax.broadcasted_iota(jnp.int32, (tk, tq), 0)
                qcol = lax.broadcasted_iota(jnp.int32, (tk, tq), 1)
                st = jnp.where(krow <= qcol, st, NEG)
            fq = fq_ref[hh:hh + 1, :]
            m_old = m_ref[hh]
            m_new = jnp.maximum(m_old, jnp.max(st, axis=0, keepdims=True) + fq)
            alpha = jnp.exp(m_old - m_new)
            pm = jnp.exp(st + (fq - m_new))
            l_ref[hh] = alpha * l_ref[hh] + jnp.sum(pm, axis=0, keepdims=True)
            acc_ref[hh] = alpha * acc_ref[hh] + _dot(vt, pm.astype(BF))
            m_ref[hh] = m_new

    @pl.when(kj < qi)
    def _():
        step(False)

    @pl.when(kj == qi)
    def _():
        step(True)
        sub = lax.broadcasted_iota(jnp.int32, (LANES, tq), 0)
        ot = jnp.where(sub < FOX_HD, acc_ref[0] / l_ref[0], acc_ref[1] / l_ref[1])
        o_ref[...] = ot.T.astype(o_ref.dtype)


def _fox_prompt(q, k, vt, cum, batch, seq, tq):
    nq = seq // tq
    pairs = [(i, j) for i in range(nq) for j in range(i + 1)]
    qi_tab = jnp.asarray([p[0] for p in pairs], jnp.int32)
    kj_tab = jnp.asarray([p[1] for p in pairs], jnp.int32)
    hp_n = FOX_HEADS // 2
    f_key = cum.reshape(batch, seq, FOX_HEADS)
    f_query = cum.reshape(batch, seq, hp_n, 2).transpose(0, 2, 3, 1)
    grid_spec = pltpu.PrefetchScalarGridSpec(
        num_scalar_prefetch=2,
        grid=(batch, hp_n, len(pairs)),
        in_specs=[
            pl.BlockSpec((tq, LANES), lambda b, h, p, qi, kj: (b * nq + qi[p], h)),
            pl.BlockSpec((tq, LANES), lambda b, h, p, qi, kj: (b * nq + kj[p], h)),
            pl.BlockSpec((None, LANES, tq), lambda b, h, p, qi, kj: (b, h, kj[p])),
            pl.BlockSpec((None, None, 2, tq), lambda b, h, p, qi, kj: (b, h, 0, qi[p])),
            pl.BlockSpec((None, tq, FOX_HEADS), lambda b, h, p, qi, kj: (b, kj[p], 0)),
        ],
        out_specs=pl.BlockSpec((tq, LANES), lambda b, h, p, qi, kj: (b * nq + qi[p], h)),
        scratch_shapes=[
            pltpu.VMEM((2, tq, LANES), BF),
            pltpu.VMEM((2, 1, tq), F32),
            pltpu.VMEM((2, 1, tq), F32),
            pltpu.VMEM((2, LANES, tq), F32),
        ],
    )
    return pl.pallas_call(
        _fox_prompt_kernel,
        grid_spec=grid_spec,
        out_shape=jax.ShapeDtypeStruct((batch * seq, D_MODEL), BF),
        compiler_params=_cparams("parallel", "parallel", "arbitrary"),
        name="fox_prompt_attention",
    )(qi_tab, kj_tab, q, k, vt, f_query, f_key)


def _fox_sample_kernel(pt_ref, q_ref, kn_ref, vn_ref, lfn_ref, *refs, npg, seq):
    k_refs = refs[:npg]
    v_refs = refs[npg:2 * npg]
    lf_refs = refs[2 * npg:3 * npg]
    o_ref = refs[3 * npg]
    qbd_ref, carry_ref, fn_ref, m_ref, l_ref, acc_ref, kpad_ref, vpad_ref = refs[3 * npg + 1:]
    g = pl.program_id(1)
    C = seq * FOX_HEADS
    P = k_refs[0].shape[0]
    row = lax.broadcasted_iota(jnp.int32, (C, D_MODEL), 0)
    col = lax.broadcasted_iota(jnp.int32, (C, D_MODEL), 1)
    own_head = (col >> 6) == (row % FOX_HEADS)
    t_of = lax.broadcasted_iota(jnp.int32, (C, P), 0) // FOX_HEADS
    j_of = lax.broadcasted_iota(jnp.int32, (C, P), 1)
    kk = lax.broadcasted_iota(jnp.int32, (P, P), 0)
    jj = lax.broadcasted_iota(jnp.int32, (P, P), 1)

    def tile_heads(a):
        return jnp.concatenate([a] * seq, axis=0)

    @pl.when(g == 0)
    def _():
        q = q_ref[...]
        qrep = jnp.concatenate(
            [jnp.broadcast_to(q[t:t + 1, :], (FOX_HEADS, D_MODEL)) for t in range(seq)], axis=0)
        qbd = jnp.where(own_head, qrep, 0.0).astype(BF)
        qbd_ref[...] = qbd
        kpad_ref[...] = jnp.zeros_like(kpad_ref)
        vpad_ref[...] = jnp.zeros_like(vpad_ref)
        kpad_ref[0:seq, :] = kn_ref[...]
        vpad_ref[0:seq, :] = vn_ref[...]
        fn = _dot_exact(lfn_ref[...], jnp.where(kk <= jj, 1.0, 0.0))
        fn_c = tile_heads(fn)
        fn_col = jnp.sum(jnp.where(j_of == t_of, fn_c, 0.0), axis=-1, keepdims=True)
        fn_ref[...] = fn_col
        s = lax.dot_general(qbd, kpad_ref[...].astype(BF), _NT, preferred_element_type=F32)
        s = jnp.where(j_of <= t_of, s + fn_col - fn_c, NEG)
        m = jnp.max(s, axis=-1, keepdims=True)
        pm = jnp.exp(s - m)
        m_ref[...] = m
        l_ref[...] = jnp.sum(pm, axis=-1, keepdims=True)
        acc_ref[...] = _dot(pm.astype(BF), vpad_ref[...].astype(BF))
        carry_ref[...] = jnp.zeros_like(carry_ref)

    later = jnp.where(kk > jj, 1.0, 0.0)
    qbd = qbd_ref[...]
    fn_col = fn_ref[...]
    carry = carry_ref[...]
    scores = []
    for i in reversed(range(npg)):
        lf = lf_refs[i][...]
        r_page = _dot_exact(lf, later) + carry
        carry = carry + jnp.sum(lf, axis=-1, keepdims=True)
        s = lax.dot_general(qbd, k_refs[i][...].astype(BF), _NT, preferred_element_type=F32)
        scores.append((i, s + fn_col + tile_heads(r_page)))
    carry_ref[...] = carry
    m_old = m_ref[...]
    m_new = m_old
    for _, s in scores:
        m_new = jnp.maximum(m_new, jnp.max(s, axis=-1, keepdims=True))
    alpha = jnp.exp(m_old - m_new)
    l_new = alpha * l_ref[...]
    acc = alpha * acc_ref[...]
    for i, s in scores:
        pm = jnp.exp(s - m_new)
        l_new = l_new + jnp.sum(pm, axis=-1, keepdims=True)
        acc = acc + _dot(pm.astype(BF), v_refs[i][...].astype(BF))
    m_ref[...] = m_new
    l_ref[...] = l_new
    acc_ref[...] = acc

    @pl.when(g == pl.num_programs(1) - 1)
    def _():
        on = jnp.where(own_head, acc / l_new, 0.0)
        for t in range(seq):
            o_ref[t:t + 1, :] = jnp.sum(on[t * FOX_HEADS:(t + 1) * FOX_HEADS, :], axis=0, keepdims=True)


def _fox_sample(q, k_new, v_new, lf_new_t, cache_k, cache_v, cache_lf_t, page_table, seq, npg):
    batch, n_pages = page_table.shape
    P = cache_k.shape[1]
    groups = n_pages // npg
    C = seq * FOX_HEADS

    def page(i):
        return lambda b, g, pt: (pt[b, (groups - 1 - g) * npg + i], 0, 0)

    new_tok = pl.BlockSpec((None, seq, D_MODEL), lambda b, g, pt: (b, 0, 0))
    grid_spec = pltpu.PrefetchScalarGridSpec(
        num_scalar_prefetch=1,
        grid=(batch, groups),
        in_specs=(
            [new_tok, new_tok, new_tok,
             pl.BlockSpec((None, FOX_HEADS, P), lambda b, g, pt: (b, 0, 0))]
            + [pl.BlockSpec((None, P, D_MODEL), page(i)) for i in range(npg)]
            + [pl.BlockSpec((None, P, D_MODEL), page(i)) for i in range(npg)]
            + [pl.BlockSpec((None, FOX_HEADS, P), page(i)) for i in range(npg)]
        ),
        out_specs=new_tok,
        scratch_shapes=[
            pltpu.VMEM((C, D_MODEL), BF),
            pltpu.VMEM((FOX_HEADS, 1), F32),
            pltpu.VMEM((C, 1), F32),
            pltpu.VMEM((C, 1), F32),
            pltpu.VMEM((C, 1), F32),
            pltpu.VMEM((C, D_MODEL), F32),
            pltpu.VMEM((P, D_MODEL), F32),
            pltpu.VMEM((P, D_MODEL), F32),
        ],
    )
    return pl.pallas_call(
        functools.partial(_fox_sample_kernel, npg=npg, seq=seq),
        grid_spec=grid_spec,
        out_shape=jax.ShapeDtypeStruct((batch, seq, D_MODEL), F32),
        compiler_params=_cparams("parallel", "arbitrary"),
        name="fox_sample_attention",
    )(page_table, q, k_new, v_new, lf_new_t,
      *([cache_k] * npg), *([cache_v] * npg), *([cache_lf_t] * npg))


def _prep_weights(w_ret_in, w_ret_out, w_q, w_o, w_ffn_in, w_ffn_out, w_kvf):
    depth, _, two_f = w_ffn_in.shape
    d_ff = two_f // 2
    nf = d_ff // FFN_CHUNK
    wi = w_ffn_in.astype(BF).reshape(depth, D_MODEL, 2 * nf, FFN_CHUNK).transpose(0, 2, 1, 3)
    wo = w_ffn_out.astype(BF).reshape(depth, nf, FFN_CHUNK, D_MODEL)
    w_kvf_b = w_kvf.astype(BF)
    wf = jnp.zeros((D_MODEL, LANES), BF).at[:, :FOX_HEADS].set(w_kvf_b[:, 2 * D_MODEL:])
    return dict(
        ret_in=w_ret_in.astype(BF), ret_out=w_ret_out.astype(BF), q=w_q.astype(BF), o=w_o.astype(BF),
        ffn_in=wi, ffn_out=wo, wk=w_kvf_b[:, :D_MODEL], wv=w_kvf_b[:, D_MODEL:2 * D_MODEL], wf=wf)


def _rope_tables(pos):
    half = RET_DK // 2
    freqs = ROPE_BASE ** (-jnp.arange(half, dtype=F32) / half)
    ang = pos.astype(F32)[:, None] * freqs[None, :]
    return jnp.cos(ang), jnp.sin(ang)


def _trunk(x, mods, kvmod, mod, pos, W, norms, b_f, *, batch, seq, tm, ret_fn, attn_fn):
    ret_norm_g, attn_norm_g, q_norm_g, ffn_norm_g, kv_norm_g, k_norm_g = norms
    n_a = W["ret_in"].shape[0]
    depth = W["ffn_in"].shape[0]
    log_g = jnp.log1p(-jnp.exp2(-5.0 - jnp.arange(RET_HEADS, dtype=F32)))
    cos, sin = _rope_tables(pos)
    states = []
    shared = None
    for l in range(depth):
        sh1, sc1, g1, sh2, sc2, g2 = mods[l]
        if l < n_a:
            a, S = ret_fn(l, x, ret_norm_g[l], sh1, sc1, W["ret_in"][l], cos, sin, log_g)
            states.append(S)
            x = _out_proj(a, W["ret_out"][l], x, g1, mod, tm)
        else:
            j = l - n_a
            if shared is None:
                shared = _kvf_proj(x, kv_norm_g, kvmod[0], kvmod[1], W["wk"], W["wv"], W["wf"],
                                   b_f, k_norm_g, mod, min(tm, 512), seq)
            a = attn_fn(j, x, attn_norm_g[j], sh1, sc1, W["q"][j], q_norm_g[j], shared)
            x = _out_proj(a, W["o"][j], x, g1, mod, tm)
        x = _ffn(x, ffn_norm_g[l], sh2, sc2, g2, W["ffn_in"][l], W["ffn_out"][l], mod, tm)
    return x, jnp.stack(states), shared


def kernel(x_prompt, x_sample, cache_k, cache_v, cache_logf, state_ret, page_table, c_prompt, c_sample,
           w_mod, b_mod, ret_norm_g, w_ret_in, w_ret_out, attn_norm_g, w_q, q_norm_g, w_o, ffn_norm_g,
           w_ffn_in, w_ffn_out, kv_norm_g, w_kvmod, b_kvmod, w_kvf, b_f, k_norm_g):
    Bp, Tp, _ = x_prompt.shape
    Bd, Td, _ = x_sample.shape
    depth = w_mod.shape[0]
    n_pool, page_size = cache_k.shape[:2]
    past_len = page_table.shape[1] * page_size
    W = _prep_weights(w_ret_in, w_ret_out, w_q, w_o, w_ffn_in, w_ffn_out, w_kvf)
    norms = (ret_norm_g, attn_norm_g, q_norm_g, ffn_norm_g, kv_norm_g, k_norm_g)

    c_all = jnp.concatenate([c_prompt, c_sample], axis=0)
    mod_all = _cond_proj(c_all, w_mod, b_mod, 1536)
    kvmod_all = _cond_proj(c_all, w_kvmod[None], b_kvmod[None], 1024)[0]

    def split_mods(lo, hi, expand):
        def lay(a):
            return jnp.repeat(a, expand, axis=0)[None] if expand else a[:, None, :]
        mods = [[lay(m) for m in jnp.split(mod_all[l, lo:hi], 6, axis=-1)] for l in range(depth)]
        kvmod = [lay(m) for m in jnp.split(kvmod_all[lo:hi], 2, axis=-1)]
        return mods, kvmod

    mods_p, kvmod_p = split_mods(0, Bp, 0)
    mod_p = _Mod(Tp, per_row=False)
    tm_p = 1024

    def ret_prompt(l, x, g, sh, sc, w, cos, sin, log_g):
        proj = _nm_matmul(x, g, sh, sc, w, mod_p, tm_p, 1536, BF)
        return _ret_prompt(proj, cos, sin, log_g, Bp, Tp, 512)

    def attn_prompt(j, x, g, sh, sc, w, gq, shared):
        _, _, _, cum, kb, vb = shared
        q = _q_proj(x, g, sh, sc, w, gq, mod_p, tm_p, BF)
        vt = vb.reshape(Bp, Tp, D_MODEL).transpose(0, 2, 1)
        return _fox_prompt(q, kb, vt, cum, Bp, Tp, 512)

    y_p, st_p, (k_p, v_p, lf_p, _, _, _) = _trunk(
        x_prompt.reshape(Bp * Tp, D_MODEL), mods_p, kvmod_p, mod_p, jnp.arange(Tp), W, norms, b_f,
        batch=Bp, seq=Tp, tm=tm_p, ret_fn=ret_prompt, attn_fn=attn_prompt)

    mods_s, kvmod_s = split_mods(Bp, Bp + Bd, Td)
    Ms = Bd * Td
    mod_s = _Mod(Ms, per_row=True)
    ck = cache_k.astype(BF).reshape(n_pool, page_size, D_MODEL)
    cv = cache_v.astype(BF).reshape(n_pool, page_size, D_MODEL)
    clf_t = cache_logf.transpose(0, 2, 1)

    def ret_sample(l, x, g, sh, sc, w, cos, sin, log_g):
        proj = _nm_matmul(x, g, sh, sc, w, mod_s, Ms, 1536, F32)
        return _ret_sample(proj, state_ret[l], cos, sin, log_g, Td, 2)

    def attn_sample(j, x, g, sh, sc, w, gq, shared):
        k, v, lf = shared[:3]
        q = _q_proj(x, g, sh, sc, w, gq, mod_s, Ms, F32)
        lf_t = jnp.pad(lf.reshape(Bd, Td, FOX_HEADS).transpose(0, 2, 1),
                       ((0, 0), (0, 0), (0, page_size - Td)))
        o = _fox_sample(q.reshape(Bd, Td, D_MODEL), k.reshape(Bd, Td, D_MODEL), v.reshape(Bd, Td, D_MODEL),
                        lf_t, ck, cv, clf_t, page_table, Td, 8)
        return o.reshape(Ms, D_MODEL)

    y_s, st_s, (k_s, v_s, lf_s, _, _, _) = _trunk(
        x_sample.reshape(Ms, D_MODEL), mods_s, kvmod_s, mod_s, past_len + jnp.arange(Td), W, norms, b_f,
        batch=Bd, seq=Td, tm=Ms, ret_fn=ret_sample, attn_fn=attn_sample)

    heads = (FOX_HEADS, FOX_HD)
    return (y_p.reshape(Bp, Tp, D_MODEL), y_s.reshape(Bd, Td, D_MODEL), st_p, st_s,
            k_p.reshape(Bp, Tp, *heads), v_p.reshape(Bp, Tp, *heads), lf_p.reshape(Bp, Tp, FOX_HEADS),
            k_s.reshape(Bd, Td, *heads), v_s.reshape(Bd, Td, *heads), lf_s.reshape(Bd, Td, FOX_HEADS))
```

```python
import functools
import math

import jax
import jax.numpy as jnp
from jax import lax
from jax.experimental import pallas as pl
from jax.experimental.pallas import tpu as pltpu

BF = jnp.bfloat16
F32 = jnp.float32

D_MODEL = 1024
RET_HEADS = 4
RET_DK = 256
RET_DV = 512
RET_CHUNK = 128
ROPE_BASE = 10000.0
FOX_HEADS = 16
FOX_HD = 64
EPS = 1e-6
LANES = 128
VMEM_LIMIT = 52 * 1024 * 1024
NEG = -1e30
FOX_PROMPT_HEADS_PER_STEP = 4

_NT = (((1,), (1,)), ((), ()))
_TN = (((0,), (0,)), ((), ()))


def _cparams(*sem):
    return pltpu.CompilerParams(dimension_semantics=sem, vmem_limit_bytes=VMEM_LIMIT)


def _dot(a, b):
    return jnp.dot(a, b, preferred_element_type=F32)


def _dot_exact(a, b):
    return jnp.dot(a, b, preferred_element_type=F32, precision=lax.Precision.HIGHEST)


def _silu(x):
    return x * jax.nn.sigmoid(x)


def _normmod(x, g, shift, scale):
    ms = jnp.mean(x * x, axis=-1, keepdims=True)
    y = x * lax.rsqrt(ms + EPS) * g
    return y * (1.0 + scale) + shift


def _split_bf16(x):
    hi = x.astype(BF)
    lo = (x - hi.astype(F32)).astype(BF)
    return hi, lo


def _headnorm(a):
    r = lax.broadcasted_iota(jnp.int32, (D_MODEL, LANES), 0) >> 6
    c = lax.broadcasted_iota(jnp.int32, (D_MODEL, LANES), 1)
    seg = jnp.where(r == c, 1.0, 0.0).astype(BF)
    rt = lax.broadcasted_iota(jnp.int32, (LANES, D_MODEL), 0)
    ct = lax.broadcasted_iota(jnp.int32, (LANES, D_MODEL), 1) >> 6
    seg_t = jnp.where(rt == ct, 1.0, 0.0).astype(BF)
    y_hi, y_lo = _split_bf16(a * a)
    ms = (_dot(y_hi, seg) + _dot(y_lo, seg)) * (1.0 / FOX_HD)
    rs = lax.rsqrt(ms + EPS)
    r_hi, r_lo = _split_bf16(rs)
    return a * (_dot(r_hi, seg_t) + _dot(r_lo, seg_t))


def _log_sigmoid(x):
    return jnp.minimum(x, 0.0) - jnp.log1p(jnp.exp(-jnp.abs(x)))


def _cond_kernel(c_ref, w_ref, b_ref, o_ref):
    ca = _silu(c_ref[...]).astype(BF)
    o_ref[...] = _dot(ca, w_ref[...].astype(BF)) + b_ref[...]


def _cond_proj(c, w, b, tn):
    L, K, N = w.shape
    R = c.shape[0]
    return pl.pallas_call(
        _cond_kernel,
        grid=(L, N // tn),
        in_specs=[
            pl.BlockSpec((R, K), lambda l, j: (0, 0)),
            pl.BlockSpec((None, K, tn), lambda l, j: (l, 0, j)),
            pl.BlockSpec((None, 1, tn), lambda l, j: (l, 0, j)),
        ],
        out_specs=pl.BlockSpec((None, R, tn), lambda l, j: (l, 0, j)),
        out_shape=jax.ShapeDtypeStruct((L, R, N), F32),
        compiler_params=_cparams("parallel", "parallel"),
        name="cond_proj",
    )(c, w, b.reshape(L, 1, N))


class _Mod:
    def __init__(self, rows_per_group, per_row):
        self.rows_per_group = rows_per_group
        self.per_row = per_row

    def spec(self, tm):
        if self.per_row:
            return pl.BlockSpec((None, tm, D_MODEL), lambda i, *_: (0, i, 0))
        tiles = self.rows_per_group // tm
        return pl.BlockSpec((None, 1, D_MODEL), lambda i, *_: (i // tiles, 0, 0))


def _vec_spec(n):
    return pl.BlockSpec((1, n), lambda i, *_: (0, 0))


def _nm_kernel(x_ref, g_ref, sh_ref, sc_ref, w_ref, o_ref, h_ref):
    @pl.when(pl.program_id(1) == 0)
    def _():
        h_ref[...] = _normmod(x_ref[...], g_ref[...], sh_ref[...], sc_ref[...]).astype(BF)

    o_ref[...] = _dot(h_ref[...], w_ref[...]).astype(o_ref.dtype)


def _nm_matmul(x, g, shift, scale, w, mod, tm, tn, out_dtype):
    M, K = x.shape
    N = w.shape[1]
    return pl.pallas_call(
        _nm_kernel,
        grid=(M // tm, N // tn),
        in_specs=[
            pl.BlockSpec((tm, K), lambda i, j: (i, 0)),
            _vec_spec(K),
            mod.spec(tm),
            mod.spec(tm),
            pl.BlockSpec((K, tn), lambda i, j: (0, j)),
        ],
        out_specs=pl.BlockSpec((tm, tn), lambda i, j: (i, j)),
        out_shape=jax.ShapeDtypeStruct((M, N), out_dtype),
        scratch_shapes=[pltpu.VMEM((tm, K), BF)],
        compiler_params=_cparams("parallel", "arbitrary"),
        name="norm_mod_matmul",
    )(x, g.reshape(1, K), shift, scale, w)


def _q_kernel(x_ref, g_ref, sh_ref, sc_ref, w_ref, gq_ref, o_ref):
    h = _normmod(x_ref[...], g_ref[...], sh_ref[...], sc_ref[...]).astype(BF)
    q = _headnorm(_dot(h, w_ref[...])) * gq_ref[...]
    o_ref[...] = (q * (FOX_HD ** -0.5)).astype(o_ref.dtype)


def _q_proj(x, g, shift, scale, w, gq, mod, tm, out_dtype):
    M, K = x.shape
    return pl.pallas_call(
        _q_kernel,
        grid=(M // tm,),
        in_specs=[
            pl.BlockSpec((tm, K), lambda i: (i, 0)),
            _vec_spec(K),
            mod.spec(tm),
            mod.spec(tm),
            pl.BlockSpec((K, D_MODEL), lambda i: (0, 0)),
            _vec_spec(D_MODEL),
        ],
        out_specs=pl.BlockSpec((tm, D_MODEL), lambda i: (i, 0)),
        out_shape=jax.ShapeDtypeStruct((M, D_MODEL), out_dtype),
        compiler_params=_cparams("parallel"),
        name="q_proj_headnorm",
    )(x, g.reshape(1, K), shift, scale, w, jnp.tile(gq, FOX_HEADS).reshape(1, D_MODEL))


def _kvf_kernel(x_ref, g_ref, sh_ref, sc_ref, wk_ref, wv_ref, wf_ref, bf_ref, gk_ref,
                k_ref, v_ref, lf_ref, cum_ref, kb_ref, vb_ref, carry_ref, *, tiles_per_seq):
    tm = x_ref.shape[0]
    h = _normmod(x_ref[...], g_ref[...], sh_ref[...], sc_ref[...]).astype(BF)
    k = _headnorm(_dot(h, wk_ref[...])) * gk_ref[...]
    v = _dot(h, wv_ref[...])
    k_ref[...] = k
    v_ref[...] = v
    kb_ref[...] = k.astype(BF)
    vb_ref[...] = v.astype(BF)
    lf = _log_sigmoid(_dot(h, wf_ref[...]) + bf_ref[...])
    lf_ref[...] = lf[:, :FOX_HEADS]

    @pl.when(pl.program_id(0) % tiles_per_seq == 0)
    def _():
        carry_ref[...] = jnp.zeros_like(carry_ref)

    r = lax.broadcasted_iota(jnp.int32, (tm, tm), 0)
    c = lax.broadcasted_iota(jnp.int32, (tm, tm), 1)
    tri = jnp.where(r >= c, 1.0, 0.0)
    cum = _dot_exact(tri, lf) + carry_ref[...]
    carry_ref[...] = cum[tm - 1:tm, :]
    cum_ref[...] = cum[:, :FOX_HEADS]


def _kvf_proj(x, g, shift, scale, wk, wv, wf, b_f, gk, mod, tm, seq_len):
    M, K = x.shape
    tok = pl.BlockSpec((tm, D_MODEL), lambda i: (i, 0))
    hd = pl.BlockSpec((tm, FOX_HEADS), lambda i: (i, 0))
    w_spec = pl.BlockSpec((K, D_MODEL), lambda i: (0, 0))
    bf_pad = jnp.zeros((1, LANES), F32).at[0, :FOX_HEADS].set(b_f)
    return pl.pallas_call(
        functools.partial(_kvf_kernel, tiles_per_seq=max(seq_len // tm, 1)),
        grid=(M // tm,),
        in_specs=[
            pl.BlockSpec((tm, K), lambda i: (i, 0)),
            _vec_spec(K),
            mod.spec(tm),
            mod.spec(tm),
            w_spec,
            w_spec,
            pl.BlockSpec((K, LANES), lambda i: (0, 0)),
            _vec_spec(LANES),
            _vec_spec(D_MODEL),
        ],
        out_specs=[tok, tok, hd, hd, tok, tok],
        out_shape=[
            jax.ShapeDtypeStruct((M, D_MODEL), F32),
            jax.ShapeDtypeStruct((M, D_MODEL), F32),
            jax.ShapeDtypeStruct((M, FOX_HEADS), F32),
            jax.ShapeDtypeStruct((M, FOX_HEADS), F32),
            jax.ShapeDtypeStruct((M, D_MODEL), BF),
            jax.ShapeDtypeStruct((M, D_MODEL), BF),
        ],
        scratch_shapes=[pltpu.VMEM((1, LANES), F32)],
        compiler_params=_cparams("arbitrary"),
        name="shared_kvf",
    )(x, g.reshape(1, K), shift, scale, wk, wv, wf, bf_pad,
      jnp.tile(gk, FOX_HEADS).reshape(1, D_MODEL))


def _out_kernel(a_ref, w_ref, x_ref, gt_ref, o_ref):
    y = _dot(a_ref[...].astype(BF), w_ref[...])
    o_ref[...] = x_ref[...] + (1.0 + gt_ref[...]) * y


def _out_proj(a, w, x, gate, mod, tm):
    M, K = a.shape
    return pl.pallas_call(
        _out_kernel,
        grid=(M // tm,),
        in_specs=[
            pl.BlockSpec((tm, K), lambda i: (i, 0)),
            pl.BlockSpec((K, D_MODEL), lambda i: (0, 0)),
            pl.BlockSpec((tm, D_MODEL), lambda i: (i, 0)),
            mod.spec(tm),
        ],
        out_specs=pl.BlockSpec((tm, D_MODEL), lambda i: (i, 0)),
        out_shape=jax.ShapeDtypeStruct((M, D_MODEL), F32),
        compiler_params=_cparams("parallel"),
        name="out_proj_residual",
    )(a, w, x, gate)


FFN_CHUNK = 256


def _ffn_kernel(x_ref, g_ref, sh_ref, sc_ref, gt_ref, wi_ref, wo_ref, o_ref, h_ref, acc_ref, *, nf):
    h_ref[...] = _normmod(x_ref[...], g_ref[...], sh_ref[...], sc_ref[...]).astype(BF)
    acc_ref[...] = jnp.zeros_like(acc_ref)

    def body(f, carry):
        h = h_ref[...]
        gate = _dot(h, wi_ref[f])
        up = _dot(h, wi_ref[nf + f])
        act = (_silu(gate) * up).astype(BF)
        acc_ref[...] += _dot(act, wo_ref[f])
        return carry

    lax.fori_loop(0, nf, body, 0)
    o_ref[...] = x_ref[...] + (1.0 + gt_ref[...]) * acc_ref[...]


def _ffn(x, g, shift, scale, gate, wi, wo, mod, tm):
    M, K = x.shape
    nf = wo.shape[0]
    resident = pl.Buffered(1)
    return pl.pallas_call(
        functools.partial(_ffn_kernel, nf=nf),
        grid=(M // tm,),
        in_specs=[
            pl.BlockSpec((tm, K), lambda i: (i, 0)),
            _vec_spec(K),
            mod.spec(tm),
            mod.spec(tm),
            mod.spec(tm),
            pl.BlockSpec(wi.shape, lambda i: (0, 0, 0), pipeline_mode=resident),
            pl.BlockSpec(wo.shape, lambda i: (0, 0, 0), pipeline_mode=resident),
        ],
        out_specs=pl.BlockSpec((tm, K), lambda i: (i, 0)),
        out_shape=jax.ShapeDtypeStruct((M, K), F32),
        scratch_shapes=[pltpu.VMEM((tm, K), BF), pltpu.VMEM((tm, K), F32)],
        compiler_params=_cparams("parallel"),
        name="ffn_swiglu",
    )(x, g.reshape(1, K), shift, scale, gate, wi, wo)


def _rotary(x, cos, sin):
    half = x.shape[-1] // 2
    x1, x2 = x[:, :half], x[:, half:]
    return jnp.concatenate([x1 * cos - x2 * sin, x1 * sin + x2 * cos], axis=-1)


def _ret_prompt_kernel(q_ref, k_ref, v_ref, g_ref, cos_ref, sin_ref, lg_ref, o_ref, s_ref, *, rows):
    C = RET_CHUNK

    @pl.when(pl.program_id(2) == 0)
    def _():
        s_ref[...] = jnp.zeros_like(s_ref)

    lg = lg_ref[...]
    i_v = lax.broadcasted_iota(jnp.int32, (C, RET_DV), 0).astype(F32)
    i_k = lax.broadcasted_iota(jnp.int32, (C, RET_DK), 0).astype(F32)
    cross = jnp.exp(lg * (i_v + 1.0))
    kdec = jnp.exp(lg_ref[:, :RET_DK] * (C - 1.0 - i_k))
    cdec = jnp.exp(lg * float(C))
    ii = lax.broadcasted_iota(jnp.int32, (C, C), 0)
    jj = lax.broadcasted_iota(jnp.int32, (C, C), 1)
    diff = (ii - jj).astype(F32)
    inner = jnp.where(diff >= 0, jnp.exp(lg_ref[:, :C] * jnp.maximum(diff, 0.0)), 0.0)

    for c in range(rows // C):
        sl = slice(c * C, (c + 1) * C)
        cos, sin = cos_ref[sl, :], sin_ref[sl, :]
        q = _rotary(q_ref[sl, :].astype(F32), cos, sin)
        k = _rotary(k_ref[sl, :].astype(F32), cos, sin) * (RET_DK ** -0.5)
        v = v_ref[sl, :]
        qb = q.astype(BF)
        S = s_ref[...]
        sc = lax.dot_general(qb, k.astype(BF), _NT, preferred_element_type=F32) * inner
        o = _dot(sc.astype(BF), v) + _dot(qb, S.astype(BF)) * cross
        s_ref[...] = S * cdec + lax.dot_general((k * kdec).astype(BF), v, _TN,
                                                preferred_element_type=F32)
        o = o * lax.rsqrt(jnp.mean(o * o, axis=-1, keepdims=True) + EPS)
        o_ref[sl, :] = (o * _silu(g_ref[sl, :].astype(F32))).astype(o_ref.dtype)


def _ret_prompt(proj, cos, sin, log_g, batch, seq, rows):
    nr = seq // rows
    kq = D_MODEL // RET_DK
    kv = 2 * D_MODEL // RET_DV
    kg = 4 * D_MODEL // RET_DV
    row = lambda b, h, r: b * nr + r
    lg = jnp.broadcast_to(log_g[:, None, None], (RET_HEADS, 1, RET_DV))
    return pl.pallas_call(
        functools.partial(_ret_prompt_kernel, rows=rows),
        grid=(batch, RET_HEADS, nr),
        in_specs=[
            pl.BlockSpec((rows, RET_DK), lambda b, h, r: (row(b, h, r), h)),
            pl.BlockSpec((rows, RET_DK), lambda b, h, r: (row(b, h, r), kq + h)),
            pl.BlockSpec((rows, RET_DV), lambda b, h, r: (row(b, h, r), kv + h)),
            pl.BlockSpec((rows, RET_DV), lambda b, h, r: (row(b, h, r), kg + h)),
            pl.BlockSpec((rows, RET_DK // 2), lambda b, h, r: (r, 0)),
            pl.BlockSpec((rows, RET_DK // 2), lambda b, h, r: (r, 0)),
            pl.BlockSpec((None, 1, RET_DV), lambda b, h, r: (h, 0, 0)),
        ],
        out_specs=[
            pl.BlockSpec((rows, RET_DV), lambda b, h, r: (row(b, h, r), h)),
            pl.BlockSpec((None, None, RET_DK, RET_DV), lambda b, h, r: (b, h, 0, 0)),
        ],
        out_shape=[
            jax.ShapeDtypeStruct((batch * seq, RET_HEADS * RET_DV), BF),
            jax.ShapeDtypeStruct((batch, RET_HEADS, RET_DK, RET_DV), F32),
        ],
        compiler_params=_cparams("parallel", "parallel", "arbitrary"),
        name="retention_prompt",
    )(proj, proj, proj, proj, cos, sin, lg)


def _ret_sample_kernel(p_ref, s0_ref, cos_ref, sin_ref, lg_ref, o_ref, s_ref, *, nb, seq):
    R = nb * seq
    ri = lax.broadcasted_iota(jnp.int32, (R, R), 0)
    rj = lax.broadcasted_iota(jnp.int32, (R, R), 1)
    same = (ri // seq) == (rj // seq)
    diff = (ri - rj).astype(F32)
    t_row = (lax.broadcasted_iota(jnp.int32, (R, 1), 0) % seq).astype(F32)
    b_row = lax.broadcasted_iota(jnp.int32, (R, 1), 0) // seq
    cos, sin = cos_ref[...], sin_ref[...]
    for h in range(RET_HEADS):
        lg = lg_ref[h]
        inner = jnp.where(same & (diff >= 0), jnp.exp(lg[:, :R] * jnp.maximum(diff, 0.0)), 0.0)
        cross = jnp.exp(lg * (t_row + 1.0))
        kdec = jnp.exp(lg[:, :RET_DK] * (seq - 1.0 - t_row))
        cdec = jnp.exp(lg * float(seq))
        q = _rotary(p_ref[:, h * RET_DK:(h + 1) * RET_DK], cos, sin)
        k = _rotary(p_ref[:, D_MODEL + h * RET_DK:D_MODEL + (h + 1) * RET_DK], cos, sin)
        k = k * (RET_DK ** -0.5)
        v = p_ref[:, 2 * D_MODEL + h * RET_DV:2 * D_MODEL + (h + 1) * RET_DV]
        g = p_ref[:, 4 * D_MODEL + h * RET_DV:4 * D_MODEL + (h + 1) * RET_DV]
        qb, vb = q.astype(BF), v.astype(BF)
        sc = lax.dot_general(qb, k.astype(BF), _NT, preferred_element_type=F32) * inner
        o = _dot(sc.astype(BF), vb)
        kd = k * kdec
        for b in range(nb):
            S = s0_ref[b, h]
            mine = b_row == b
            o = o + jnp.where(mine, _dot(qb, S.astype(BF)) * cross, 0.0)
            kb = jnp.where(mine, kd, 0.0).astype(BF)
            s_ref[b, h] = S * cdec + lax.dot_general(kb, vb, _TN, preferred_element_type=F32)
        o = o * lax.rsqrt(jnp.mean(o * o, axis=-1, keepdims=True) + EPS)
        o_ref[:, h * RET_DV:(h + 1) * RET_DV] = o * _silu(g)


def _ret_sample(proj, s0, cos, sin, log_g, seq, nb):
    M = proj.shape[0]
    batch = M // seq
    R = nb * seq
    lg = jnp.broadcast_to(log_g[:, None, None], (RET_HEADS, 1, RET_DV))
    state = pl.BlockSpec((nb, RET_HEADS, RET_DK, RET_DV), lambda i: (i, 0, 0, 0))
    return pl.pallas_call(
        functools.partial(_ret_sample_kernel, nb=nb, seq=seq),
        grid=(batch // nb,),
        in_specs=[
            pl.BlockSpec((R, 6 * D_MODEL), lambda i: (i, 0)),
            state,
            pl.BlockSpec((R, RET_DK // 2), lambda i: (0, 0)),
            pl.BlockSpec((R, RET_DK // 2), lambda i: (0, 0)),
            pl.BlockSpec((RET_HEADS, 1, RET_DV), lambda i: (0, 0, 0)),
        ],
        out_specs=[pl.BlockSpec((R, RET_HEADS * RET_DV), lambda i: (i, 0)), state],
        out_shape=[
            jax.ShapeDtypeStruct((M, RET_HEADS * RET_DV), F32),
            jax.ShapeDtypeStruct(s0.shape, F32),
        ],
        compiler_params=_cparams("parallel"),
        name="retention_sample",
    )(proj, s0, jnp.tile(cos, (nb, 1)), jnp.tile(sin, (nb, 1)), lg)


def _fox_prompt_kernel(qi_ref, kj_ref, q_ref, k_ref, vt_ref, fq_ref, fk_ref, o_ref,
                       qm_ref, m_ref, l_ref, acc_ref):
    tq = q_ref.shape[0]
    tk = k_ref.shape[0]
    nh = m_ref.shape[0]
    p = pl.program_id(2)
    hg = pl.program_id(1)
    qi = qi_ref[p]
    kj = kj_ref[p]

    def lanes(hh):
        return slice((hh // 2) * LANES, (hh // 2 + 1) * LANES)

    @pl.when(kj == 0)
    def _():
        lane = lax.broadcasted_iota(jnp.int32, (tq, LANES), 1)
        for hh in range(nh):
            q = q_ref[:, lanes(hh)].astype(F32)
            mine = (lane < FOX_HD) if hh % 2 == 0 else (lane >= FOX_HD)
            qm_ref[hh] = jnp.where(mine, q, 0.0).astype(BF)
        m_ref[...] = jnp.full_like(m_ref, NEG)
        l_ref[...] = jnp.zeros_like(l_ref)
        acc_ref[...] = jnp.zeros_like(acc_ref)

    def step(diagonal):
        fk = fk_ref[...]
        head = lax.broadcasted_iota(jnp.int32, fk.shape, 1)
        for hh in range(nh):
            k = k_ref[:, lanes(hh)]
            vt = vt_ref[lanes(hh), :]
            fk_col = jnp.sum(jnp.where(head == nh * hg + hh, fk, 0.0), axis=1, keepdims=True)
            st = lax.dot_general(k, qm_ref[hh], _NT, preferred_element_type=F32) - fk_col
            if diagonal:
                krow = lax.broadcasted_iota(jnp.int32, (tk, tq), 0)
                qcol = lax.broadcasted_iota(jnp.int32, (tk, tq), 1)
                st = jnp.where(krow <= qcol, st, NEG)
            fq = fq_ref[hh:hh + 1, :]
            m_old = m_ref[hh]
            m_new = jnp.maximum(m_old, jnp.max(st, axis=0, keepdims=True) + fq)
            alpha = jnp.exp(m_old - m_new)
            pm = jnp.exp(st + (fq - m_new))
            l_ref[hh] = alpha * l_ref[hh] + jnp.sum(pm, axis=0, keepdims=True)
            acc_ref[hh] = alpha * acc_ref[hh] + _dot(vt, pm.astype(BF))
            m_ref[hh] = m_new

    @pl.when(kj < qi)
    def _():
        step(False)

    @pl.when(kj == qi)
    def _():
        step(True)
        sub = lax.broadcasted_iota(jnp.int32, (LANES, tq), 0)
        for hh in range(0, nh, 2):
            ot = jnp.where(sub < FOX_HD, acc_ref[hh] / l_ref[hh], acc_ref[hh + 1] / l_ref[hh + 1])
            o_ref[:, lanes(hh)] = ot.T.astype(o_ref.dtype)


def _fox_prompt(q, k, vt, cum, batch, seq, tq):
    nq = seq // tq
    pairs = [(i, j) for i in range(nq) for j in range(i + 1)]
    qi_tab = jnp.asarray([p[0] for p in pairs], jnp.int32)
    kj_tab = jnp.asarray([p[1] for p in pairs], jnp.int32)
    nh = FOX_PROMPT_HEADS_PER_STEP
    groups = FOX_HEADS // nh
    w = nh * FOX_HD
    f_key = cum.reshape(batch, seq, FOX_HEADS)
    f_query = cum.reshape(batch, seq, groups, nh).transpose(0, 2, 3, 1)
    grid_spec = pltpu.PrefetchScalarGridSpec(
        num_scalar_prefetch=2,
        grid=(batch, groups, len(pairs)),
        in_specs=[
            pl.BlockSpec((tq, w), lambda b, h, p, qi, kj: (b * nq + qi[p], h)),
            pl.BlockSpec((tq, w), lambda b, h, p, qi, kj: (b * nq + kj[p], h)),
            pl.BlockSpec((None, w, tq), lambda b, h, p, qi, kj: (b, h, kj[p])),
            pl.BlockSpec((None, None, nh, tq), lambda b, h, p, qi, kj: (b, h, 0, qi[p])),
            pl.BlockSpec((None, tq, FOX_HEADS), lambda b, h, p, qi, kj: (b, kj[p], 0)),
        ],
        out_specs=pl.BlockSpec((tq, w), lambda b, h, p, qi, kj: (b * nq + qi[p], h)),
        scratch_shapes=[
            pltpu.VMEM((nh, tq, LANES), BF),
            pltpu.VMEM((nh, 1, tq), F32),
            pltpu.VMEM((nh, 1, tq), F32),
            pltpu.VMEM((nh, LANES, tq), F32),
        ],
    )
    return pl.pallas_call(
        _fox_prompt_kernel,
        grid_spec=grid_spec,
        out_shape=jax.ShapeDtypeStruct((batch * seq, D_MODEL), BF),
        compiler_params=_cparams("parallel", "parallel", "arbitrary"),
        name="fox_prompt_attention",
    )(qi_tab, kj_tab, q, k, vt, f_query, f_key)


def _gather_past_kernel(pt_ref, *refs, npg):
    k_refs = refs[:npg]
    v_refs = refs[npg:2 * npg]
    lf_refs = refs[2 * npg:3 * npg]
    kc_ref, vc_ref, r_ref, carry_ref = refs[3 * npg:]
    P = k_refs[0].shape[0]

    @pl.when(pl.program_id(1) == 0)
    def _():
        carry_ref[...] = jnp.zeros_like(carry_ref)

    kk = lax.broadcasted_iota(jnp.int32, (P, P), 0)
    jj = lax.broadcasted_iota(jnp.int32, (P, P), 1)
    later = jnp.where(kk > jj, 1.0, 0.0)
    carry = carry_ref[...]
    for i in reversed(range(npg)):
        lf = lf_refs[i][...]
        r_ref[:, i * P:(i + 1) * P] = _dot_exact(lf, later) + carry
        carry = carry + jnp.sum(lf, axis=-1, keepdims=True)
        kc_ref[i * P:(i + 1) * P, :] = k_refs[i][...].astype(BF).reshape(P, D_MODEL)
        vc_ref[i * P:(i + 1) * P, :] = v_refs[i][...].astype(BF).reshape(P, D_MODEL)
    carry_ref[...] = carry


def _gather_past(cache_k, cache_v, cache_lf_t, page_table, npg):
    batch, n_pages = page_table.shape
    P = cache_k.shape[1]
    groups = n_pages // npg
    past = n_pages * P

    def page(i, nd):
        return lambda b, g, pt: (pt[b, (groups - 1 - g) * npg + i],) + (0,) * nd

    kv_page = lambda i: pl.BlockSpec((None, P, FOX_HEADS, FOX_HD), page(i, 3))
    kv_out = pl.BlockSpec((None, npg * P, D_MODEL), lambda b, g, pt: (b, groups - 1 - g, 0))
    grid_spec = pltpu.PrefetchScalarGridSpec(
        num_scalar_prefetch=1,
        grid=(batch, groups),
        in_specs=([kv_page(i) for i in range(npg)] + [kv_page(i) for i in range(npg)]
                  + [pl.BlockSpec((None, FOX_HEADS, P), page(i, 2)) for i in range(npg)]),
        out_specs=[kv_out, kv_out,
                   pl.BlockSpec((None, FOX_HEADS, npg * P), lambda b, g, pt: (b, 0, groups - 1 - g))],
        scratch_shapes=[pltpu.VMEM((FOX_HEADS, 1), F32)],
    )
    return pl.pallas_call(
        functools.partial(_gather_past_kernel, npg=npg),
        grid_spec=grid_spec,
        out_shape=[
            jax.ShapeDtypeStruct((batch, past, D_MODEL), BF),
            jax.ShapeDtypeStruct((batch, past, D_MODEL), BF),
            jax.ShapeDtypeStruct((batch, FOX_HEADS, past), F32),
        ],
        compiler_params=_cparams("parallel", "arbitrary"),
        name="gather_past",
    )(page_table, *([cache_k] * npg), *([cache_v] * npg), *([cache_lf_t] * npg))


def _fox_sample_kernel(q_ref, kn_ref, vn_ref, lfn_ref, kc_ref, vc_ref, r_ref, o_ref,
                       qbd_ref, fn_ref, m_ref, l_ref, acc_ref, kpad_ref, vpad_ref, *, seq):
    g = pl.program_id(1)
    C = seq * FOX_HEADS
    P = kpad_ref.shape[0]
    row = lax.broadcasted_iota(jnp.int32, (C, D_MODEL), 0)
    col = lax.broadcasted_iota(jnp.int32, (C, D_MODEL), 1)
    own_head = (col >> 6) == (row % FOX_HEADS)
    t_of = lax.broadcasted_iota(jnp.int32, (C, P), 0) // FOX_HEADS
    j_of = lax.broadcasted_iota(jnp.int32, (C, P), 1)
    kk = lax.broadcasted_iota(jnp.int32, (P, P), 0)
    jj = lax.broadcasted_iota(jnp.int32, (P, P), 1)

    def tile_heads(a):
        return jnp.concatenate([a] * seq, axis=0)

    @pl.when(g == 0)
    def _():
        q = q_ref[...]
        qrep = jnp.concatenate(
            [jnp.broadcast_to(q[t:t + 1, :], (FOX_HEADS, D_MODEL)) for t in range(seq)], axis=0)
        qbd = jnp.where(own_head, qrep, 0.0).astype(BF)
        qbd_ref[...] = qbd
        kpad_ref[...] = jnp.zeros_like(kpad_ref)
        vpad_ref[...] = jnp.zeros_like(vpad_ref)
        kpad_ref[0:seq, :] = kn_ref[...]
        vpad_ref[0:seq, :] = vn_ref[...]
        fn = _dot_exact(lfn_ref[...], jnp.where(kk <= jj, 1.0, 0.0))
        fn_c = tile_heads(fn)
        fn_col = jnp.sum(jnp.where(j_of == t_of, fn_c, 0.0), axis=-1, keepdims=True)
        fn_ref[...] = fn_col
        s = lax.dot_general(qbd, kpad_ref[...].astype(BF), _NT, preferred_element_type=F32)
        s = jnp.where(j_of <= t_of, s + fn_col - fn_c, NEG)
        m = jnp.max(s, axis=-1, keepdims=True)
        pm = jnp.exp(s - m)
        m_ref[...] = m
        l_ref[...] = jnp.sum(pm, axis=-1, keepdims=True)
        acc_ref[...] = _dot(pm.astype(BF), vpad_ref[...].astype(BF))

    s = lax.dot_general(qbd_ref[...], kc_ref[...], _NT, preferred_element_type=F32)
    s = s + fn_ref[...] + tile_heads(r_ref[...])
    m_old = m_ref[...]
    m_new = jnp.maximum(m_old, jnp.max(s, axis=-1, keepdims=True))
    alpha = jnp.exp(m_old - m_new)
    pm = jnp.exp(s - m_new)
    l_new = alpha * l_ref[...] + jnp.sum(pm, axis=-1, keepdims=True)
    acc = alpha * acc_ref[...] + _dot(pm.astype(BF), vc_ref[...])
    m_ref[...] = m_new
    l_ref[...] = l_new
    acc_ref[...] = acc

    @pl.when(g == pl.num_programs(1) - 1)
    def _():
        on = jnp.where(own_head, acc / l_new, 0.0)
        for t in range(seq):
            o_ref[t:t + 1, :] = jnp.sum(on[t * FOX_HEADS:(t + 1) * FOX_HEADS, :], axis=0, keepdims=True)


def _fox_sample(q, k_new, v_new, lf_new_t, kc, vc, r_past, seq, tk):
    batch, past, _ = kc.shape
    P = lf_new_t.shape[-1]
    C = seq * FOX_HEADS
    new_tok = pl.BlockSpec((None, seq, D_MODEL), lambda b, g: (b, 0, 0))
    past_blk = pl.BlockSpec((None, tk, D_MODEL), lambda b, g: (b, g, 0))
    return pl.pallas_call(
        functools.partial(_fox_sample_kernel, seq=seq),
        grid=(batch, past // tk),
        in_specs=[new_tok, new_tok, new_tok,
                  pl.BlockSpec((None, FOX_HEADS, P), lambda b, g: (b, 0, 0)),
                  past_blk, past_blk,
                  pl.BlockSpec((None, FOX_HEADS, tk), lambda b, g: (b, 0, g))],
        out_specs=new_tok,
        out_shape=jax.ShapeDtypeStruct((batch, seq, D_MODEL), F32),
        scratch_shapes=[
            pltpu.VMEM((C, D_MODEL), BF),
            pltpu.VMEM((C, 1), F32),
            pltpu.VMEM((C, 1), F32),
            pltpu.VMEM((C, 1), F32),
            pltpu.VMEM((C, D_MODEL), F32),
            pltpu.VMEM((P, D_MODEL), F32),
            pltpu.VMEM((P, D_MODEL), F32),
        ],
        compiler_params=_cparams("parallel", "arbitrary"),
        name="fox_sample_attention",
    )(q, k_new, v_new, lf_new_t, kc, vc, r_past)


def _prep_weights(w_ret_in, w_ret_out, w_q, w_o, w_ffn_in, w_ffn_out, w_kvf):
    depth, _, two_f = w_ffn_in.shape
    d_ff = two_f // 2
    nf = d_ff // FFN_CHUNK
    wi = w_ffn_in.astype(BF).reshape(depth, D_MODEL, 2 * nf, FFN_CHUNK).transpose(0, 2, 1, 3)
    wo = w_ffn_out.astype(BF).reshape(depth, nf, FFN_CHUNK, D_MODEL)
    w_kvf_b = w_kvf.astype(BF)
    wf = jnp.zeros((D_MODEL, LANES), BF).at[:, :FOX_HEADS].set(w_kvf_b[:, 2 * D_MODEL:])
    return dict(
        ret_in=w_ret_in.astype(BF), ret_out=w_ret_out.astype(BF), q=w_q.astype(BF), o=w_o.astype(BF),
        ffn_in=wi, ffn_out=wo, wk=w_kvf_b[:, :D_MODEL], wv=w_kvf_b[:, D_MODEL:2 * D_MODEL], wf=wf)


def _rope_tables(pos):
    half = RET_DK // 2
    freqs = ROPE_BASE ** (-jnp.arange(half, dtype=F32) / half)
    ang = pos.astype(F32)[:, None] * freqs[None, :]
    return jnp.cos(ang), jnp.sin(ang)


def _trunk(x, mods, kvmod, mod, pos, W, norms, b_f, *, batch, seq, tm, ret_fn, attn_fn):
    ret_norm_g, attn_norm_g, q_norm_g, ffn_norm_g, kv_norm_g, k_norm_g = norms
    n_a = W["ret_in"].shape[0]
    depth = W["ffn_in"].shape[0]
    log_g = jnp.log1p(-jnp.exp2(-5.0 - jnp.arange(RET_HEADS, dtype=F32)))
    cos, sin = _rope_tables(pos)
    states = []
    shared = None
    for l in range(depth):
        sh1, sc1, g1, sh2, sc2, g2 = mods[l]
        if l < n_a:
            a, S = ret_fn(l, x, ret_norm_g[l], sh1, sc1, W["ret_in"][l], cos, sin, log_g)
            states.append(S)
            x = _out_proj(a, W["ret_out"][l], x, g1, mod, tm)
        else:
            j = l - n_a
            if shared is None:
                shared = _kvf_proj(x, kv_norm_g, kvmod[0], kvmod[1], W["wk"], W["wv"], W["wf"],
                                   b_f, k_norm_g, mod, min(tm, 512), seq)
            a = attn_fn(j, x, attn_norm_g[j], sh1, sc1, W["q"][j], q_norm_g[j], shared)
            x = _out_proj(a, W["o"][j], x, g1, mod, tm)
        x = _ffn(x, ffn_norm_g[l], sh2, sc2, g2, W["ffn_in"][l], W["ffn_out"][l], mod, tm)
    return x, jnp.stack(states), shared


def kernel(x_prompt, x_sample, cache_k, cache_v, cache_logf, state_ret, page_table, c_prompt, c_sample,
           w_mod, b_mod, ret_norm_g, w_ret_in, w_ret_out, attn_norm_g, w_q, q_norm_g, w_o, ffn_norm_g,
           w_ffn_in, w_ffn_out, kv_norm_g, w_kvmod, b_kvmod, w_kvf, b_f, k_norm_g):
    Bp, Tp, _ = x_prompt.shape
    Bd, Td, _ = x_sample.shape
    depth = w_mod.shape[0]
    n_pool, page_size = cache_k.shape[:2]
    past_len = page_table.shape[1] * page_size
    W = _prep_weights(w_ret_in, w_ret_out, w_q, w_o, w_ffn_in, w_ffn_out, w_kvf)
    norms = (ret_norm_g, attn_norm_g, q_norm_g, ffn_norm_g, kv_norm_g, k_norm_g)

    c_all = jnp.concatenate([c_prompt, c_sample], axis=0)
    mod_all = _cond_proj(c_all, w_mod, b_mod, 1536)
    kvmod_all = _cond_proj(c_all, w_kvmod[None], b_kvmod[None], 1024)[0]

    def split_mods(lo, hi, expand):
        def lay(a):
            return jnp.repeat(a, expand, axis=0)[None] if expand else a[:, None, :]
        mods = [[lay(m) for m in jnp.split(mod_all[l, lo:hi], 6, axis=-1)] for l in range(depth)]
        kvmod = [lay(m) for m in jnp.split(kvmod_all[lo:hi], 2, axis=-1)]
        return mods, kvmod

    mods_p, kvmod_p = split_mods(0, Bp, 0)
    mod_p = _Mod(Tp, per_row=False)
    tm_p = 1024

    def ret_prompt(l, x, g, sh, sc, w, cos, sin, log_g):
        proj = _nm_matmul(x, g, sh, sc, w, mod_p, tm_p, 1536, BF)
        return _ret_prompt(proj, cos, sin, log_g, Bp, Tp, 512)

    def attn_prompt(j, x, g, sh, sc, w, gq, shared):
        _, _, _, cum, kb, vb = shared
        q = _q_proj(x, g, sh, sc, w, gq, mod_p, tm_p, BF)
        vt = vb.reshape(Bp, Tp, D_MODEL).transpose(0, 2, 1)
        return _fox_prompt(q, kb, vt, cum, Bp, Tp, 512)

    y_p, st_p, (k_p, v_p, lf_p, _, _, _) = _trunk(
        x_prompt.reshape(Bp * Tp, D_MODEL), mods_p, kvmod_p, mod_p, jnp.arange(Tp), W, norms, b_f,
        batch=Bp, seq=Tp, tm=tm_p, ret_fn=ret_prompt, attn_fn=attn_prompt)

    mods_s, kvmod_s = split_mods(Bp, Bp + Bd, Td)
    Ms = Bd * Td
    mod_s = _Mod(Ms, per_row=True)
    kc, vc, r_past = _gather_past(cache_k, cache_v, cache_logf.transpose(0, 2, 1), page_table, 4)

    def ret_sample(l, x, g, sh, sc, w, cos, sin, log_g):
        proj = _nm_matmul(x, g, sh, sc, w, mod_s, Ms, 1536, F32)
        return _ret_sample(proj, state_ret[l], cos, sin, log_g, Td, 2)

    def attn_sample(j, x, g, sh, sc, w, gq, shared):
        k, v, lf = shared[:3]
        q = _q_proj(x, g, sh, sc, w, gq, mod_s, Ms, F32)
        lf_t = jnp.pad(lf.reshape(Bd, Td, FOX_HEADS).transpose(0, 2, 1),
                       ((0, 0), (0, 0), (0, page_size - Td)))
        o = _fox_sample(q.reshape(Bd, Td, D_MODEL), k.reshape(Bd, Td, D_MODEL), v.reshape(Bd, Td, D_MODEL),
                        lf_t, kc, vc, r_past, Td, 1024)
        return o.reshape(Ms, D_MODEL)

    y_s, st_s, (k_s, v_s, lf_s, _, _, _) = _trunk(
        x_sample.reshape(Ms, D_MODEL), mods_s, kvmod_s, mod_s, past_len + jnp.arange(Td), W, norms, b_f,
        batch=Bd, seq=Td, tm=Ms, ret_fn=ret_sample, attn_fn=attn_sample)

    heads = (FOX_HEADS, FOX_HD)
    return (y_p.reshape(Bp, Tp, D_MODEL), y_s.reshape(Bd, Td, D_MODEL), st_p, st_s,
            k_p.reshape(Bp, Tp, *heads), v_p.reshape(Bp, Tp, *heads), lf_p.reshape(Bp, Tp, FOX_HEADS),
            k_s.reshape(Bd, Td, *heads), v_s.reshape(Bd, Td, *heads), lf_s.reshape(Bd, Td, FOX_HEADS))
```

```python
import functools
import math

import jax
import jax.numpy as jnp
from jax import lax
from jax.experimental import pallas as pl
from jax.experimental.pallas import tpu as pltpu

BF = jnp.bfloat16
F32 = jnp.float32

D_MODEL = 1024
RET_HEADS = 4
RET_DK = 256
RET_DV = 512
RET_CHUNK = 128
ROPE_BASE = 10000.0
FOX_HEADS = 16
FOX_HD = 64
EPS = 1e-6
LANES = 128
VMEM_LIMIT = 52 * 1024 * 1024
NEG = -1e30
FOX_PROMPT_HEADS_PER_STEP = 4

_NT = (((1,), (1,)), ((), ()))
_TN = (((0,), (0,)), ((), ()))


def _cparams(*sem):
    return pltpu.CompilerParams(dimension_semantics=sem, vmem_limit_bytes=VMEM_LIMIT)


def _dot(a, b):
    return jnp.dot(a, b, preferred_element_type=F32)


def _dot_exact(a, b):
    return jnp.dot(a, b, preferred_element_type=F32, precision=lax.Precision.HIGHEST)


def _silu(x):
    return x * jax.nn.sigmoid(x)


def _normmod(x, g, shift, scale):
    ms = jnp.mean(x * x, axis=-1, keepdims=True)
    y = x * lax.rsqrt(ms + EPS) * g
    return y * (1.0 + scale) + shift


def _split_bf16(x):
    hi = x.astype(BF)
    lo = (x - hi.astype(F32)).astype(BF)
    return hi, lo


def _headnorm(a):
    r = lax.broadcasted_iota(jnp.int32, (D_MODEL, LANES), 0) >> 6
    c = lax.broadcasted_iota(jnp.int32, (D_MODEL, LANES), 1)
    seg = jnp.where(r == c, 1.0, 0.0).astype(BF)
    rt = lax.broadcasted_iota(jnp.int32, (LANES, D_MODEL), 0)
    ct = lax.broadcasted_iota(jnp.int32, (LANES, D_MODEL), 1) >> 6
    seg_t = jnp.where(rt == ct, 1.0, 0.0).astype(BF)
    y_hi, y_lo = _split_bf16(a * a)
    ms = (_dot(y_hi, seg) + _dot(y_lo, seg)) * (1.0 / FOX_HD)
    rs = lax.rsqrt(ms + EPS)
    r_hi, r_lo = _split_bf16(rs)
    return a * (_dot(r_hi, seg_t) + _dot(r_lo, seg_t))


def _log_sigmoid(x):
    return jnp.minimum(x, 0.0) - jnp.log1p(jnp.exp(-jnp.abs(x)))


def _cond_kernel(c_ref, w_ref, b_ref, o_ref):
    ca = _silu(c_ref[...]).astype(BF)
    o_ref[...] = _dot(ca, w_ref[...].astype(BF)) + b_ref[...]


def _cond_proj(c, w, b, tn):
    L, K, N = w.shape
    R = c.shape[0]
    return pl.pallas_call(
        _cond_kernel,
        grid=(L, N // tn),
        in_specs=[
            pl.BlockSpec((R, K), lambda l, j: (0, 0)),
            pl.BlockSpec((None, K, tn), lambda l, j: (l, 0, j)),
            pl.BlockSpec((None, 1, tn), lambda l, j: (l, 0, j)),
        ],
        out_specs=pl.BlockSpec((None, R, tn), lambda l, j: (l, 0, j)),
        out_shape=jax.ShapeDtypeStruct((L, R, N), F32),
        compiler_params=_cparams("parallel", "parallel"),
        name="cond_proj",
    )(c, w, b.reshape(L, 1, N))


class _Mod:
    def __init__(self, rows_per_group, per_row):
        self.rows_per_group = rows_per_group
        self.per_row = per_row

    def spec(self, tm):
        if self.per_row:
            return pl.BlockSpec((None, tm, D_MODEL), lambda i, *_: (0, i, 0))
        tiles = self.rows_per_group // tm
        return pl.BlockSpec((None, 1, D_MODEL), lambda i, *_: (i // tiles, 0, 0))


def _vec_spec(n):
    return pl.BlockSpec((1, n), lambda i, *_: (0, 0))


def _nm_kernel(x_ref, g_ref, sh_ref, sc_ref, w_ref, o_ref, h_ref):
    @pl.when(pl.program_id(1) == 0)
    def _():
        h_ref[...] = _normmod(x_ref[...], g_ref[...], sh_ref[...], sc_ref[...]).astype(BF)

    o_ref[...] = _dot(h_ref[...], w_ref[...]).astype(o_ref.dtype)


def _nm_matmul(x, g, shift, scale, w, mod, tm, tn, out_dtype):
    M, K = x.shape
    N = w.shape[1]
    return pl.pallas_call(
        _nm_kernel,
        grid=(M // tm, N // tn),
        in_specs=[
            pl.BlockSpec((tm, K), lambda i, j: (i, 0)),
            _vec_spec(K),
            mod.spec(tm),
            mod.spec(tm),
            pl.BlockSpec((K, tn), lambda i, j: (0, j)),
        ],
        out_specs=pl.BlockSpec((tm, tn), lambda i, j: (i, j)),
        out_shape=jax.ShapeDtypeStruct((M, N), out_dtype),
        scratch_shapes=[pltpu.VMEM((tm, K), BF)],
        compiler_params=_cparams("parallel", "arbitrary"),
        name="norm_mod_matmul",
    )(x, g.reshape(1, K), shift, scale, w)


def _q_kernel(x_ref, g_ref, sh_ref, sc_ref, w_ref, gq_ref, o_ref):
    h = _normmod(x_ref[...], g_ref[...], sh_ref[...], sc_ref[...]).astype(BF)
    q = _headnorm(_dot(h, w_ref[...])) * gq_ref[...]
    o_ref[...] = (q * (FOX_HD ** -0.5)).astype(o_ref.dtype)


def _q_proj(x, g, shift, scale, w, gq, mod, tm, out_dtype):
    M, K = x.shape
    return pl.pallas_call(
        _q_kernel,
        grid=(M // tm,),
        in_specs=[
            pl.BlockSpec((tm, K), lambda i: (i, 0)),
            _vec_spec(K),
            mod.spec(tm),
            mod.spec(tm),
            pl.BlockSpec((K, D_MODEL), lambda i: (0, 0)),
            _vec_spec(D_MODEL),
        ],
        out_specs=pl.BlockSpec((tm, D_MODEL), lambda i: (i, 0)),
        out_shape=jax.ShapeDtypeStruct((M, D_MODEL), out_dtype),
        compiler_params=_cparams("parallel"),
        name="q_proj_headnorm",
    )(x, g.reshape(1, K), shift, scale, w, jnp.tile(gq, FOX_HEADS).reshape(1, D_MODEL))


def _kvf_core(x_ref, g_ref, sh_ref, sc_ref, wk_ref, wv_ref, wf_ref, bf_ref, gk_ref):
    h = _normmod(x_ref[...], g_ref[...], sh_ref[...], sc_ref[...]).astype(BF)
    k = _headnorm(_dot(h, wk_ref[...])) * gk_ref[...]
    v = _dot(h, wv_ref[...])
    lf = _log_sigmoid(_dot(h, wf_ref[...]) + bf_ref[...])
    return k, v, lf


def _kvf_rows_kernel(*refs):
    k_ref, v_ref, lf_ref = refs[9:]
    k, v, lf = _kvf_core(*refs[:9])
    k_ref[...] = k
    v_ref[...] = v
    lf_ref[...] = lf[:, :FOX_HEADS]


def _kvf_seq_kernel(*refs, tiles_per_seq):
    kt_ref, vt_ref, lft_ref, cumt_ref, cum_ref, kb_ref, vtb_ref, carry_ref = refs[9:]
    tm = kb_ref.shape[0]
    k, v, lf = _kvf_core(*refs[:9])
    kt_ref[...] = k.T
    vt = v.T
    vt_ref[...] = vt
    vtb_ref[...] = vt.astype(BF)
    kb_ref[...] = k.astype(BF)
    lft_ref[...] = lf.T[:FOX_HEADS, :]

    @pl.when(pl.program_id(0) % tiles_per_seq == 0)
    def _():
        carry_ref[...] = jnp.zeros_like(carry_ref)

    r = lax.broadcasted_iota(jnp.int32, (tm, tm), 0)
    c = lax.broadcasted_iota(jnp.int32, (tm, tm), 1)
    tri = jnp.where(r >= c, 1.0, 0.0)
    cum = _dot_exact(tri, lf) + carry_ref[...]
    carry_ref[...] = cum[tm - 1:tm, :]
    cum_ref[...] = cum[:, :FOX_HEADS]
    cumt_ref[...] = cum.T[:FOX_HEADS, :]


def _kvf_proj(x, g, shift, scale, wk, wv, wf, b_f, gk, mod, tm, seq_len=None):
    M, K = x.shape
    tok = pl.BlockSpec((tm, D_MODEL), lambda i: (i, 0))
    hd = pl.BlockSpec((tm, FOX_HEADS), lambda i: (i, 0))
    w_spec = pl.BlockSpec((K, D_MODEL), lambda i: (0, 0))
    bf_pad = jnp.zeros((1, LANES), F32).at[0, :FOX_HEADS].set(b_f)
    in_specs = [
        pl.BlockSpec((tm, K), lambda i: (i, 0)),
        _vec_spec(K),
        mod.spec(tm),
        mod.spec(tm),
        w_spec,
        w_spec,
        pl.BlockSpec((K, LANES), lambda i: (0, 0)),
        _vec_spec(LANES),
        _vec_spec(D_MODEL),
    ]
    args = (x, g.reshape(1, K), shift, scale, wk, wv, wf, bf_pad, jnp.tile(gk, FOX_HEADS).reshape(1, D_MODEL))
    if seq_len is None:
        return pl.pallas_call(
            _kvf_rows_kernel,
            grid=(M // tm,),
            in_specs=in_specs,
            out_specs=[tok, tok, hd],
            out_shape=[
                jax.ShapeDtypeStruct((M, D_MODEL), F32),
                jax.ShapeDtypeStruct((M, D_MODEL), F32),
                jax.ShapeDtypeStruct((M, FOX_HEADS), F32),
            ],
            compiler_params=_cparams("parallel"),
            name="shared_kvf_rows",
        )(*args)
    tiles = seq_len // tm
    batch = M // seq_len
    feat_t = pl.BlockSpec((None, D_MODEL, tm), lambda i: (i // tiles, 0, i % tiles))
    head_t = pl.BlockSpec((None, FOX_HEADS, tm), lambda i: (i // tiles, 0, i % tiles))
    return pl.pallas_call(
        functools.partial(_kvf_seq_kernel, tiles_per_seq=tiles),
        grid=(M // tm,),
        in_specs=in_specs,
        out_specs=[feat_t, feat_t, head_t, head_t, hd, tok, feat_t],
        out_shape=[
            jax.ShapeDtypeStruct((batch, D_MODEL, seq_len), F32),
            jax.ShapeDtypeStruct((batch, D_MODEL, seq_len), F32),
            jax.ShapeDtypeStruct((batch, FOX_HEADS, seq_len), F32),
            jax.ShapeDtypeStruct((batch, FOX_HEADS, seq_len), F32),
            jax.ShapeDtypeStruct((M, FOX_HEADS), F32),
            jax.ShapeDtypeStruct((M, D_MODEL), BF),
            jax.ShapeDtypeStruct((batch, D_MODEL, seq_len), BF),
        ],
        scratch_shapes=[pltpu.VMEM((1, LANES), F32)],
        compiler_params=_cparams("arbitrary"),
        name="shared_kvf_seq",
    )(*args)


def _out_kernel(a_ref, w_ref, x_ref, gt_ref, o_ref):
    y = _dot(a_ref[...].astype(BF), w_ref[...])
    o_ref[...] = x_ref[...] + (1.0 + gt_ref[...]) * y


def _out_proj(a, w, x, gate, mod, tm):
    M, K = a.shape
    return pl.pallas_call(
        _out_kernel,
        grid=(M // tm,),
        in_specs=[
            pl.BlockSpec((tm, K), lambda i: (i, 0)),
            pl.BlockSpec((K, D_MODEL), lambda i: (0, 0)),
            pl.BlockSpec((tm, D_MODEL), lambda i: (i, 0)),
            mod.spec(tm),
        ],
        out_specs=pl.BlockSpec((tm, D_MODEL), lambda i: (i, 0)),
        out_shape=jax.ShapeDtypeStruct((M, D_MODEL), F32),
        compiler_params=_cparams("parallel"),
        name="out_proj_residual",
    )(a, w, x, gate)


FFN_CHUNK = 256


def _ffn_kernel(x_ref, g_ref, sh_ref, sc_ref, gt_ref, wi_ref, wo_ref, o_ref, h_ref, acc_ref, *, nf):
    h_ref[...] = _normmod(x_ref[...], g_ref[...], sh_ref[...], sc_ref[...]).astype(BF)
    acc_ref[...] = jnp.zeros_like(acc_ref)

    def body(f, carry):
        h = h_ref[...]
        gate = _dot(h, wi_ref[f])
        up = _dot(h, wi_ref[nf + f])
        act = (_silu(gate) * up).astype(BF)
        acc_ref[...] += _dot(act, wo_ref[f])
        return carry

    lax.fori_loop(0, nf, body, 0)
    o_ref[...] = x_ref[...] + (1.0 + gt_ref[...]) * acc_ref[...]


def _ffn(x, g, shift, scale, gate, wi, wo, mod, tm):
    M, K = x.shape
    nf = wo.shape[0]
    resident = pl.Buffered(1)
    return pl.pallas_call(
        functools.partial(_ffn_kernel, nf=nf),
        grid=(M // tm,),
        in_specs=[
            pl.BlockSpec((tm, K), lambda i: (i, 0)),
            _vec_spec(K),
            mod.spec(tm),
            mod.spec(tm),
            mod.spec(tm),
            pl.BlockSpec(wi.shape, lambda i: (0, 0, 0), pipeline_mode=resident),
            pl.BlockSpec(wo.shape, lambda i: (0, 0, 0), pipeline_mode=resident),
        ],
        out_specs=pl.BlockSpec((tm, K), lambda i: (i, 0)),
        out_shape=jax.ShapeDtypeStruct((M, K), F32),
        scratch_shapes=[pltpu.VMEM((tm, K), BF), pltpu.VMEM((tm, K), F32)],
        compiler_params=_cparams("parallel"),
        name="ffn_swiglu",
    )(x, g.reshape(1, K), shift, scale, gate, wi, wo)


def _rotary(x, cos, sin):
    half = x.shape[-1] // 2
    x1, x2 = x[:, :half], x[:, half:]
    return jnp.concatenate([x1 * cos - x2 * sin, x1 * sin + x2 * cos], axis=-1)


def _ret_prompt_kernel(q_ref, k_ref, v_ref, g_ref, cos_ref, sin_ref, lg_ref, o_ref, s_ref, *, rows):
    C = RET_CHUNK

    @pl.when(pl.program_id(2) == 0)
    def _():
        s_ref[...] = jnp.zeros_like(s_ref)

    lg = lg_ref[...]
    i_v = lax.broadcasted_iota(jnp.int32, (C, RET_DV), 0).astype(F32)
    i_k = lax.broadcasted_iota(jnp.int32, (C, RET_DK), 0).astype(F32)
    cross = jnp.exp(lg * (i_v + 1.0))
    kdec = jnp.exp(lg_ref[:, :RET_DK] * (C - 1.0 - i_k))
    cdec = jnp.exp(lg * float(C))
    ii = lax.broadcasted_iota(jnp.int32, (C, C), 0)
    jj = lax.broadcasted_iota(jnp.int32, (C, C), 1)
    diff = (ii - jj).astype(F32)
    inner = jnp.where(diff >= 0, jnp.exp(lg_ref[:, :C] * jnp.maximum(diff, 0.0)), 0.0)

    for c in range(rows // C):
        sl = slice(c * C, (c + 1) * C)
        cos, sin = cos_ref[sl, :], sin_ref[sl, :]
        q = _rotary(q_ref[sl, :].astype(F32), cos, sin)
        k = _rotary(k_ref[sl, :].astype(F32), cos, sin) * (RET_DK ** -0.5)
        v = v_ref[sl, :]
        qb = q.astype(BF)
        S = s_ref[...]
        sc = lax.dot_general(qb, k.astype(BF), _NT, preferred_element_type=F32) * inner
        o = _dot(sc.astype(BF), v) + _dot(qb, S.astype(BF)) * cross
        s_ref[...] = S * cdec + lax.dot_general((k * kdec).astype(BF), v, _TN,
                                                preferred_element_type=F32)
        o = o * lax.rsqrt(jnp.mean(o * o, axis=-1, keepdims=True) + EPS)
        o_ref[sl, :] = (o * _silu(g_ref[sl, :].astype(F32))).astype(o_ref.dtype)


def _ret_prompt(proj, cos, sin, log_g, batch, seq, rows):
    nr = seq // rows
    kq = D_MODEL // RET_DK
    kv = 2 * D_MODEL // RET_DV
    kg = 4 * D_MODEL // RET_DV
    row = lambda b, h, r: b * nr + r
    lg = jnp.broadcast_to(log_g[:, None, None], (RET_HEADS, 1, RET_DV))
    return pl.pallas_call(
        functools.partial(_ret_prompt_kernel, rows=rows),
        grid=(batch, RET_HEADS, nr),
        in_specs=[
            pl.BlockSpec((rows, RET_DK), lambda b, h, r: (row(b, h, r), h)),
            pl.BlockSpec((rows, RET_DK), lambda b, h, r: (row(b, h, r), kq + h)),
            pl.BlockSpec((rows, RET_DV), lambda b, h, r: (row(b, h, r), kv + h)),
            pl.BlockSpec((rows, RET_DV), lambda b, h, r: (row(b, h, r), kg + h)),
            pl.BlockSpec((rows, RET_DK // 2), lambda b, h, r: (r, 0)),
            pl.BlockSpec((rows, RET_DK // 2), lambda b, h, r: (r, 0)),
            pl.BlockSpec((None, 1, RET_DV), lambda b, h, r: (h, 0, 0)),
        ],
        out_specs=[
            pl.BlockSpec((rows, RET_DV), lambda b, h, r: (row(b, h, r), h)),
            pl.BlockSpec((None, None, RET_DK, RET_DV), lambda b, h, r: (b, h, 0, 0)),
        ],
        out_shape=[
            jax.ShapeDtypeStruct((batch * seq, RET_HEADS * RET_DV), BF),
            jax.ShapeDtypeStruct((batch, RET_HEADS, RET_DK, RET_DV), F32),
        ],
        compiler_params=_cparams("parallel", "parallel", "arbitrary"),
        name="retention_prompt",
    )(proj, proj, proj, proj, cos, sin, lg)


def _ret_sample_kernel(p_ref, s0_ref, cos_ref, sin_ref, lg_ref, o_ref, s_ref, *, nb, seq):
    R = nb * seq
    ri = lax.broadcasted_iota(jnp.int32, (R, R), 0)
    rj = lax.broadcasted_iota(jnp.int32, (R, R), 1)
    same = (ri // seq) == (rj // seq)
    diff = (ri - rj).astype(F32)
    t_row = (lax.broadcasted_iota(jnp.int32, (R, 1), 0) % seq).astype(F32)
    b_row = lax.broadcasted_iota(jnp.int32, (R, 1), 0) // seq
    cos, sin = cos_ref[...], sin_ref[...]
    for h in range(RET_HEADS):
        lg = lg_ref[h]
        inner = jnp.where(same & (diff >= 0), jnp.exp(lg[:, :R] * jnp.maximum(diff, 0.0)), 0.0)
        cross = jnp.exp(lg * (t_row + 1.0))
        kdec = jnp.exp(lg[:, :RET_DK] * (seq - 1.0 - t_row))
        cdec = jnp.exp(lg * float(seq))
        q = _rotary(p_ref[:, h * RET_DK:(h + 1) * RET_DK], cos, sin)
        k = _rotary(p_ref[:, D_MODEL + h * RET_DK:D_MODEL + (h + 1) * RET_DK], cos, sin)
        k = k * (RET_DK ** -0.5)
        v = p_ref[:, 2 * D_MODEL + h * RET_DV:2 * D_MODEL + (h + 1) * RET_DV]
        g = p_ref[:, 4 * D_MODEL + h * RET_DV:4 * D_MODEL + (h + 1) * RET_DV]
        qb, vb = q.astype(BF), v.astype(BF)
        sc = lax.dot_general(qb, k.astype(BF), _NT, preferred_element_type=F32) * inner
        o = _dot(sc.astype(BF), vb)
        kd = k * kdec
        for b in range(nb):
            S = s0_ref[b, h]
            mine = b_row == b
            o = o + jnp.where(mine, _dot(qb, S.astype(BF)) * cross, 0.0)
            kb = jnp.where(mine, kd, 0.0).astype(BF)
            s_ref[b, h] = S * cdec + lax.dot_general(kb, vb, _TN, preferred_element_type=F32)
        o = o * lax.rsqrt(jnp.mean(o * o, axis=-1, keepdims=True) + EPS)
        o_ref[:, h * RET_DV:(h + 1) * RET_DV] = o * _silu(g)


def _ret_sample(proj, s0, cos, sin, log_g, seq, nb):
    M = proj.shape[0]
    batch = M // seq
    R = nb * seq
    lg = jnp.broadcast_to(log_g[:, None, None], (RET_HEADS, 1, RET_DV))
    state = pl.BlockSpec((nb, RET_HEADS, RET_DK, RET_DV), lambda i: (i, 0, 0, 0))
    return pl.pallas_call(
        functools.partial(_ret_sample_kernel, nb=nb, seq=seq),
        grid=(batch // nb,),
        in_specs=[
            pl.BlockSpec((R, 6 * D_MODEL), lambda i: (i, 0)),
            state,
            pl.BlockSpec((R, RET_DK // 2), lambda i: (0, 0)),
            pl.BlockSpec((R, RET_DK // 2), lambda i: (0, 0)),
            pl.BlockSpec((RET_HEADS, 1, RET_DV), lambda i: (0, 0, 0)),
        ],
        out_specs=[pl.BlockSpec((R, RET_HEADS * RET_DV), lambda i: (i, 0)), state],
        out_shape=[
            jax.ShapeDtypeStruct((M, RET_HEADS * RET_DV), F32),
            jax.ShapeDtypeStruct(s0.shape, F32),
        ],
        compiler_params=_cparams("parallel"),
        name="retention_sample",
    )(proj, s0, jnp.tile(cos, (nb, 1)), jnp.tile(sin, (nb, 1)), lg)


def _fox_prompt_kernel(qi_ref, kj_ref, q_ref, k_ref, vt_ref, fq_ref, fk_ref, o_ref,
                       qm_ref, m_ref, l_ref, acc_ref):
    tq = q_ref.shape[0]
    tk = k_ref.shape[0]
    nh = m_ref.shape[0]
    p = pl.program_id(2)
    hg = pl.program_id(1)
    qi = qi_ref[p]
    kj = kj_ref[p]

    def lanes(hh):
        return slice((hh // 2) * LANES, (hh // 2 + 1) * LANES)

    @pl.when(kj == 0)
    def _():
        lane = lax.broadcasted_iota(jnp.int32, (tq, LANES), 1)
        for hh in range(nh):
            q = q_ref[:, lanes(hh)].astype(F32)
            mine = (lane < FOX_HD) if hh % 2 == 0 else (lane >= FOX_HD)
            qm_ref[hh] = jnp.where(mine, q, 0.0).astype(BF)
        m_ref[...] = jnp.full_like(m_ref, NEG)
        l_ref[...] = jnp.zeros_like(l_ref)
        acc_ref[...] = jnp.zeros_like(acc_ref)

    def step(diagonal):
        fk = fk_ref[...]
        head = lax.broadcasted_iota(jnp.int32, fk.shape, 1)
        for hh in range(nh):
            k = k_ref[:, lanes(hh)]
            vt = vt_ref[lanes(hh), :]
            fk_col = jnp.sum(jnp.where(head == nh * hg + hh, fk, 0.0), axis=1, keepdims=True)
            st = lax.dot_general(k, qm_ref[hh], _NT, preferred_element_type=F32) - fk_col
            if diagonal:
                krow = lax.broadcasted_iota(jnp.int32, (tk, tq), 0)
                qcol = lax.broadcasted_iota(jnp.int32, (tk, tq), 1)
                st = jnp.where(krow <= qcol, st, NEG)
            fq = fq_ref[hh:hh + 1, :]
            m_old = m_ref[hh]
            m_new = jnp.maximum(m_old, jnp.max(st, axis=0, keepdims=True) + fq)
            alpha = jnp.exp(m_old - m_new)
            pm = jnp.exp(st + (fq - m_new))
            l_ref[hh] = alpha * l_ref[hh] + jnp.sum(pm, axis=0, keepdims=True)
            acc_ref[hh] = alpha * acc_ref[hh] + _dot(vt, pm.astype(BF))
            m_ref[hh] = m_new

    @pl.when(kj < qi)
    def _():
        step(False)

    @pl.when(kj == qi)
    def _():
        step(True)
        sub = lax.broadcasted_iota(jnp.int32, (LANES, tq), 0)
        for hh in range(0, nh, 2):
            ot = jnp.where(sub < FOX_HD, acc_ref[hh] / l_ref[hh], acc_ref[hh + 1] / l_ref[hh + 1])
            o_ref[:, lanes(hh)] = ot.T.astype(o_ref.dtype)


def _fox_prompt(q, k, vt, cum, cum_t, batch, seq, tq):
    nq = seq // tq
    pairs = [(i, j) for i in range(nq) for j in range(i + 1)]
    qi_tab = jnp.asarray([p[0] for p in pairs], jnp.int32)
    kj_tab = jnp.asarray([p[1] for p in pairs], jnp.int32)
    nh = FOX_PROMPT_HEADS_PER_STEP
    groups = FOX_HEADS // nh
    w = nh * FOX_HD
    f_key = cum.reshape(batch, seq, FOX_HEADS)
    f_query = cum_t.reshape(batch, groups, nh, seq)
    grid_spec = pltpu.PrefetchScalarGridSpec(
        num_scalar_prefetch=2,
        grid=(batch, groups, len(pairs)),
        in_specs=[
            pl.BlockSpec((tq, w), lambda b, h, p, qi, kj: (b * nq + qi[p], h)),
            pl.BlockSpec((tq, w), lambda b, h, p, qi, kj: (b * nq + kj[p], h)),
            pl.BlockSpec((None, w, tq), lambda b, h, p, qi, kj: (b, h, kj[p])),
            pl.BlockSpec((None, None, nh, tq), lambda b, h, p, qi, kj: (b, h, 0, qi[p])),
            pl.BlockSpec((None, tq, FOX_HEADS), lambda b, h, p, qi, kj: (b, kj[p], 0)),
        ],
        out_specs=pl.BlockSpec((tq, w), lambda b, h, p, qi, kj: (b * nq + qi[p], h)),
        scratch_shapes=[
            pltpu.VMEM((nh, tq, LANES), BF),
            pltpu.VMEM((nh, 1, tq), F32),
            pltpu.VMEM((nh, 1, tq), F32),
            pltpu.VMEM((nh, LANES, tq), F32),
        ],
    )
    return pl.pallas_call(
        _fox_prompt_kernel,
        grid_spec=grid_spec,
        out_shape=jax.ShapeDtypeStruct((batch * seq, D_MODEL), BF),
        compiler_params=_cparams("parallel", "parallel", "arbitrary"),
        name="fox_prompt_attention",
    )(qi_tab, kj_tab, q, k, vt, f_query, f_key)


def _fox_sample_kernel(pt_ref, q_ref, kn_ref, vn_ref, lfn_ref, *refs, npg, seq):
    kt_refs = refs[:npg]
    vt_refs = refs[npg:2 * npg]
    lf_refs = refs[2 * npg:3 * npg]
    o_ref = refs[3 * npg]
    qbd_ref, carry_ref, fn_ref, m_ref, l_ref, acc_ref, kpad_ref, vpad_ref = refs[3 * npg + 1:]
    g = pl.program_id(1)
    C = seq * FOX_HEADS
    P = kpad_ref.shape[0]
    row = lax.broadcasted_iota(jnp.int32, (C, D_MODEL), 0)
    col = lax.broadcasted_iota(jnp.int32, (C, D_MODEL), 1)
    own_head = (col >> 6) == (row % FOX_HEADS)
    t_of = lax.broadcasted_iota(jnp.int32, (C, P), 0) // FOX_HEADS
    j_of = lax.broadcasted_iota(jnp.int32, (C, P), 1)
    kk = lax.broadcasted_iota(jnp.int32, (P, P), 0)
    jj = lax.broadcasted_iota(jnp.int32, (P, P), 1)

    def tile_heads(a):
        return jnp.concatenate([a] * seq, axis=0)

    @pl.when(g == 0)
    def _():
        q = q_ref[...]
        qrep = jnp.concatenate(
            [jnp.broadcast_to(q[t:t + 1, :], (FOX_HEADS, D_MODEL)) for t in range(seq)], axis=0)
        qbd = jnp.where(own_head, qrep, 0.0).astype(BF)
        qbd_ref[...] = qbd
        kpad_ref[...] = jnp.zeros_like(kpad_ref)
        vpad_ref[...] = jnp.zeros_like(vpad_ref)
        kpad_ref[0:seq, :] = kn_ref[...]
        vpad_ref[0:seq, :] = vn_ref[...]
        fn = _dot_exact(lfn_ref[...], jnp.where(kk <= jj, 1.0, 0.0))
        fn_c = tile_heads(fn)
        fn_col = jnp.sum(jnp.where(j_of == t_of, fn_c, 0.0), axis=-1, keepdims=True)
        fn_ref[...] = fn_col
        s = lax.dot_general(qbd, kpad_ref[...].astype(BF), _NT, preferred_element_type=F32)
        s = jnp.where(j_of <= t_of, s + fn_col - fn_c, NEG)
        m = jnp.max(s, axis=-1, keepdims=True)
        pm = jnp.exp(s - m)
        m_ref[...] = m
        l_ref[...] = jnp.sum(pm, axis=-1, keepdims=True)
        acc_ref[...] = _dot(pm.astype(BF), vpad_ref[...].astype(BF))
        carry_ref[...] = jnp.zeros_like(carry_ref)

    later = jnp.where(kk > jj, 1.0, 0.0)
    qbd = qbd_ref[...]
    fn_col = fn_ref[...]
    carry = carry_ref[...]
    scores = []
    for i in reversed(range(npg)):
        lf = lf_refs[i][...]
        r_page = _dot_exact(lf, later) + carry
        carry = carry + jnp.sum(lf, axis=-1, keepdims=True)
        s = _dot(qbd, kt_refs[i][...].astype(BF))
        scores.append((i, s + fn_col + tile_heads(r_page)))
    carry_ref[...] = carry
    m_old = m_ref[...]
    m_new = m_old
    for _, s in scores:
        m_new = jnp.maximum(m_new, jnp.max(s, axis=-1, keepdims=True))
    alpha = jnp.exp(m_old - m_new)
    l_new = alpha * l_ref[...]
    acc = alpha * acc_ref[...]
    for i, s in scores:
        pm = jnp.exp(s - m_new)
        l_new = l_new + jnp.sum(pm, axis=-1, keepdims=True)
        acc = acc + lax.dot_general(pm.astype(BF), vt_refs[i][...].astype(BF), _NT,
                                    preferred_element_type=F32)
    m_ref[...] = m_new
    l_ref[...] = l_new
    acc_ref[...] = acc

    @pl.when(g == pl.num_programs(1) - 1)
    def _():
        on = jnp.where(own_head, acc / l_new, 0.0)
        for t in range(seq):
            o_ref[t:t + 1, :] = jnp.sum(on[t * FOX_HEADS:(t + 1) * FOX_HEADS, :], axis=0, keepdims=True)


def _fox_sample(q, k_new, v_new, lf_new_t, cache_kt, cache_vt, cache_lf_t, page_table, seq, npg):
    batch, n_pages = page_table.shape
    P = cache_kt.shape[-1]
    groups = n_pages // npg
    C = seq * FOX_HEADS

    def page(i):
        return lambda b, g, pt: (pt[b, (groups - 1 - g) * npg + i], 0, 0)

    new_tok = pl.BlockSpec((None, seq, D_MODEL), lambda b, g, pt: (b, 0, 0))
    grid_spec = pltpu.PrefetchScalarGridSpec(
        num_scalar_prefetch=1,
        grid=(batch, groups),
        in_specs=(
            [new_tok, new_tok, new_tok,
             pl.BlockSpec((None, FOX_HEADS, P), lambda b, g, pt: (b, 0, 0))]
            + [pl.BlockSpec((None, D_MODEL, P), page(i)) for i in range(npg)]
            + [pl.BlockSpec((None, D_MODEL, P), page(i)) for i in range(npg)]
            + [pl.BlockSpec((None, FOX_HEADS, P), page(i)) for i in range(npg)]
        ),
        out_specs=new_tok,
        scratch_shapes=[
            pltpu.VMEM((C, D_MODEL), BF),
            pltpu.VMEM((FOX_HEADS, 1), F32),
            pltpu.VMEM((C, 1), F32),
            pltpu.VMEM((C, 1), F32),
            pltpu.VMEM((C, 1), F32),
            pltpu.VMEM((C, D_MODEL), F32),
            pltpu.VMEM((P, D_MODEL), F32),
            pltpu.VMEM((P, D_MODEL), F32),
        ],
    )
    return pl.pallas_call(
        functools.partial(_fox_sample_kernel, npg=npg, seq=seq),
        grid_spec=grid_spec,
        out_shape=jax.ShapeDtypeStruct((batch, seq, D_MODEL), F32),
        compiler_params=_cparams("parallel", "arbitrary"),
        name="fox_sample_attention",
    )(page_table, q, k_new, v_new, lf_new_t,
      *([cache_kt] * npg), *([cache_vt] * npg), *([cache_lf_t] * npg))


def _prep_weights(w_ret_in, w_ret_out, w_q, w_o, w_ffn_in, w_ffn_out, w_kvf):
    depth, _, two_f = w_ffn_in.shape
    d_ff = two_f // 2
    nf = d_ff // FFN_CHUNK
    wi = w_ffn_in.astype(BF).reshape(depth, D_MODEL, 2 * nf, FFN_CHUNK).transpose(0, 2, 1, 3)
    wo = w_ffn_out.astype(BF).reshape(depth, nf, FFN_CHUNK, D_MODEL)
    w_kvf_b = w_kvf.astype(BF)
    wf = jnp.zeros((D_MODEL, LANES), BF).at[:, :FOX_HEADS].set(w_kvf_b[:, 2 * D_MODEL:])
    return dict(
        ret_in=w_ret_in.astype(BF), ret_out=w_ret_out.astype(BF), q=w_q.astype(BF), o=w_o.astype(BF),
        ffn_in=wi, ffn_out=wo, wk=w_kvf_b[:, :D_MODEL], wv=w_kvf_b[:, D_MODEL:2 * D_MODEL], wf=wf)


def _rope_tables(pos):
    half = RET_DK // 2
    freqs = ROPE_BASE ** (-jnp.arange(half, dtype=F32) / half)
    ang = pos.astype(F32)[:, None] * freqs[None, :]
    return jnp.cos(ang), jnp.sin(ang)


def _trunk(x, mods, kvmod, mod, pos, W, norms, b_f, *, tm, kvf_seq, ret_fn, attn_fn):
    ret_norm_g, attn_norm_g, q_norm_g, ffn_norm_g, kv_norm_g, k_norm_g = norms
    n_a = W["ret_in"].shape[0]
    depth = W["ffn_in"].shape[0]
    log_g = jnp.log1p(-jnp.exp2(-5.0 - jnp.arange(RET_HEADS, dtype=F32)))
    cos, sin = _rope_tables(pos)
    states = []
    shared = None
    for l in range(depth):
        sh1, sc1, g1, sh2, sc2, g2 = mods[l]
        if l < n_a:
            a, S = ret_fn(l, x, ret_norm_g[l], sh1, sc1, W["ret_in"][l], cos, sin, log_g)
            states.append(S)
            x = _out_proj(a, W["ret_out"][l], x, g1, mod, tm)
        else:
            j = l - n_a
            if shared is None:
                shared = _kvf_proj(x, kv_norm_g, kvmod[0], kvmod[1], W["wk"], W["wv"], W["wf"],
                                   b_f, k_norm_g, mod, min(tm, 512), kvf_seq)
            a = attn_fn(j, x, attn_norm_g[j], sh1, sc1, W["q"][j], q_norm_g[j], shared)
            x = _out_proj(a, W["o"][j], x, g1, mod, tm)
        x = _ffn(x, ffn_norm_g[l], sh2, sc2, g2, W["ffn_in"][l], W["ffn_out"][l], mod, tm)
    return x, jnp.stack(states), shared


def kernel(x_prompt, x_sample, cache_k, cache_v, cache_logf, state_ret, page_table, c_prompt, c_sample,
           w_mod, b_mod, ret_norm_g, w_ret_in, w_ret_out, attn_norm_g, w_q, q_norm_g, w_o, ffn_norm_g,
           w_ffn_in, w_ffn_out, kv_norm_g, w_kvmod, b_kvmod, w_kvf, b_f, k_norm_g):
    Bp, Tp, _ = x_prompt.shape
    Bd, Td, _ = x_sample.shape
    depth = w_mod.shape[0]
    n_pool, page_size = cache_k.shape[:2]
    past_len = page_table.shape[1] * page_size
    W = _prep_weights(w_ret_in, w_ret_out, w_q, w_o, w_ffn_in, w_ffn_out, w_kvf)
    norms = (ret_norm_g, attn_norm_g, q_norm_g, ffn_norm_g, kv_norm_g, k_norm_g)

    c_all = jnp.concatenate([c_prompt, c_sample], axis=0)
    mod_all = _cond_proj(c_all, w_mod, b_mod, 1536)
    kvmod_all = _cond_proj(c_all, w_kvmod[None], b_kvmod[None], 1024)[0]

    def split_mods(lo, hi, expand):
        def lay(a):
            return jnp.repeat(a, expand, axis=0)[None] if expand else a[:, None, :]
        mods = [[lay(m) for m in jnp.split(mod_all[l, lo:hi], 6, axis=-1)] for l in range(depth)]
        kvmod = [lay(m) for m in jnp.split(kvmod_all[lo:hi], 2, axis=-1)]
        return mods, kvmod

    mods_p, kvmod_p = split_mods(0, Bp, 0)
    mod_p = _Mod(Tp, per_row=False)
    tm_p = 1024

    def ret_prompt(l, x, g, sh, sc, w, cos, sin, log_g):
        proj = _nm_matmul(x, g, sh, sc, w, mod_p, tm_p, 1536, BF)
        return _ret_prompt(proj, cos, sin, log_g, Bp, Tp, 512)

    def attn_prompt(j, x, g, sh, sc, w, gq, shared):
        _, _, _, cum_t, cum, kb, vtb = shared
        q = _q_proj(x, g, sh, sc, w, gq, mod_p, tm_p, BF)
        return _fox_prompt(q, kb, vtb, cum, cum_t, Bp, Tp, 512)

    y_p, st_p, (kt_p, vt_p, lft_p, _, _, _, _) = _trunk(
        x_prompt.reshape(Bp * Tp, D_MODEL), mods_p, kvmod_p, mod_p, jnp.arange(Tp), W, norms, b_f,
        tm=tm_p, kvf_seq=Tp, ret_fn=ret_prompt, attn_fn=attn_prompt)

    mods_s, kvmod_s = split_mods(Bp, Bp + Bd, Td)
    Ms = Bd * Td
    mod_s = _Mod(Ms, per_row=True)
    ckt = cache_k.transpose(0, 2, 3, 1).reshape(n_pool, D_MODEL, page_size)
    cvt = cache_v.transpose(0, 2, 3, 1).reshape(n_pool, D_MODEL, page_size)
    clf_t = cache_logf.transpose(0, 2, 1)

    def ret_sample(l, x, g, sh, sc, w, cos, sin, log_g):
        proj = _nm_matmul(x, g, sh, sc, w, mod_s, Ms, 1536, F32)
        return _ret_sample(proj, state_ret[l], cos, sin, log_g, Td, 2)

    def attn_sample(j, x, g, sh, sc, w, gq, shared):
        k, v, lf = shared
        q = _q_proj(x, g, sh, sc, w, gq, mod_s, Ms, F32)
        lf_t = jnp.pad(lf.reshape(Bd, Td, FOX_HEADS).transpose(0, 2, 1),
                       ((0, 0), (0, 0), (0, page_size - Td)))
        o = _fox_sample(q.reshape(Bd, Td, D_MODEL), k.reshape(Bd, Td, D_MODEL), v.reshape(Bd, Td, D_MODEL),
                        lf_t, ckt, cvt, clf_t, page_table, Td, 8)
        return o.reshape(Ms, D_MODEL)

    y_s, st_s, (k_s, v_s, lf_s) = _trunk(
        x_sample.reshape(Ms, D_MODEL), mods_s, kvmod_s, mod_s, past_len + jnp.arange(Td), W, norms, b_f,
        tm=Ms, kvf_seq=None, ret_fn=ret_sample, attn_fn=attn_sample)

    heads = (FOX_HEADS, FOX_HD)
    k_p = kt_p.reshape(Bp, *heads, Tp).transpose(0, 3, 1, 2)
    v_p = vt_p.reshape(Bp, *heads, Tp).transpose(0, 3, 1, 2)
    return (y_p.reshape(Bp, Tp, D_MODEL), y_s.reshape(Bd, Td, D_MODEL), st_p, st_s,
            k_p, v_p, lft_p.transpose(0, 2, 1),
            k_s.reshape(Bd, Td, *heads), v_s.reshape(Bd, Td, *heads), lf_s.reshape(Bd, Td, FOX_HEADS))
```

```python
import functools
import math

import jax
import jax.numpy as jnp
from jax import lax
from jax.experimental import pallas as pl
from jax.experimental.pallas import tpu as pltpu

BF = jnp.bfloat16
F32 = jnp.float32

D_MODEL = 1024
RET_HEADS = 4
RET_DK = 256
RET_DV = 512
RET_CHUNK = 128
ROPE_BASE = 10000.0
FOX_HEADS = 16
FOX_HD = 64
EPS = 1e-6
LANES = 128
VMEM_LIMIT = 52 * 1024 * 1024
NEG = -1e30
LOG2E = math.log2(math.e)

_NT = (((1,), (1,)), ((), ()))
_TN = (((0,), (0,)), ((), ()))


def _cparams(*sem):
    return pltpu.CompilerParams(dimension_semantics=sem, vmem_limit_bytes=VMEM_LIMIT)


def _dot(a, b):
    return jnp.dot(a, b, preferred_element_type=F32)


def _dot_exact(a, b):
    return jnp.dot(a, b, preferred_element_type=F32, precision=lax.Precision.HIGHEST)


def _silu(x):
    return x * jax.nn.sigmoid(x)


def _normmod(x, g, shift, scale):
    ms = jnp.mean(x * x, axis=-1, keepdims=True)
    y = x * lax.rsqrt(ms + EPS) * g
    return y * (1.0 + scale) + shift


def _split_bf16(x):
    hi = x.astype(BF)
    lo = (x - hi.astype(F32)).astype(BF)
    return hi, lo


def _headnorm(a):
    r = lax.broadcasted_iota(jnp.int32, (D_MODEL, LANES), 0) >> 6
    c = lax.broadcasted_iota(jnp.int32, (D_MODEL, LANES), 1)
    seg = jnp.where(r == c, 1.0, 0.0).astype(BF)
    rt = lax.broadcasted_iota(jnp.int32, (LANES, D_MODEL), 0)
    ct = lax.broadcasted_iota(jnp.int32, (LANES, D_MODEL), 1) >> 6
    seg_t = jnp.where(rt == ct, 1.0, 0.0).astype(BF)
    y_hi, y_lo = _split_bf16(a * a)
    ms = (_dot(y_hi, seg) + _dot(y_lo, seg)) * (1.0 / FOX_HD)
    rs = lax.rsqrt(ms + EPS)
    r_hi, r_lo = _split_bf16(rs)
    return a * (_dot(r_hi, seg_t) + _dot(r_lo, seg_t))


def _log_sigmoid(x):
    return jnp.minimum(x, 0.0) - jnp.log1p(jnp.exp(-jnp.abs(x)))


def _cond_kernel(c_ref, w_ref, b_ref, o_ref):
    ca = _silu(c_ref[...]).astype(BF)
    o_ref[...] = _dot(ca, w_ref[...].astype(BF)) + b_ref[...]


def _cond_proj(c, w, b, tn):
    L, K, N = w.shape
    R = c.shape[0]
    return pl.pallas_call(
        _cond_kernel,
        grid=(L, N // tn),
        in_specs=[
            pl.BlockSpec((R, K), lambda l, j: (0, 0)),
            pl.BlockSpec((None, K, tn), lambda l, j: (l, 0, j)),
            pl.BlockSpec((None, 1, tn), lambda l, j: (l, 0, j)),
        ],
        out_specs=pl.BlockSpec((None, R, tn), lambda l, j: (l, 0, j)),
        out_shape=jax.ShapeDtypeStruct((L, R, N), F32),
        compiler_params=_cparams("parallel", "parallel"),
        name="cond_proj",
    )(c, w, b.reshape(L, 1, N))


class _Mod:
    def __init__(self, rows_per_group, per_row):
        self.rows_per_group = rows_per_group
        self.per_row = per_row

    def spec(self, tm):
        if self.per_row:
            return pl.BlockSpec((None, tm, D_MODEL), lambda i, *_: (0, i, 0))
        tiles = self.rows_per_group // tm
        return pl.BlockSpec((None, 1, D_MODEL), lambda i, *_: (i // tiles, 0, 0))


def _vec_spec(n):
    return pl.BlockSpec((1, n), lambda i, *_: (0, 0))


def _nm_kernel(x_ref, g_ref, sh_ref, sc_ref, w_ref, o_ref, h_ref):
    @pl.when(pl.program_id(1) == 0)
    def _():
        h_ref[...] = _normmod(x_ref[...], g_ref[...], sh_ref[...], sc_ref[...]).astype(BF)

    o_ref[...] = _dot(h_ref[...], w_ref[...]).astype(o_ref.dtype)


def _nm_matmul(x, g, shift, scale, w, mod, tm, tn, out_dtype):
    M, K = x.shape
    N = w.shape[1]
    return pl.pallas_call(
        _nm_kernel,
        grid=(M // tm, N // tn),
        in_specs=[
            pl.BlockSpec((tm, K), lambda i, j: (i, 0)),
            _vec_spec(K),
            mod.spec(tm),
            mod.spec(tm),
            pl.BlockSpec((K, tn), lambda i, j: (0, j)),
        ],
        out_specs=pl.BlockSpec((tm, tn), lambda i, j: (i, j)),
        out_shape=jax.ShapeDtypeStruct((M, N), out_dtype),
        scratch_shapes=[pltpu.VMEM((tm, K), BF)],
        compiler_params=_cparams("parallel", "arbitrary"),
        name="norm_mod_matmul",
    )(x, g.reshape(1, K), shift, scale, w)


def _q_kernel(x_ref, g_ref, sh_ref, sc_ref, w_ref, gq_ref, o_ref, *, q_scale):
    h = _normmod(x_ref[...], g_ref[...], sh_ref[...], sc_ref[...]).astype(BF)
    q = _headnorm(_dot(h, w_ref[...])) * gq_ref[...]
    o_ref[...] = (q * q_scale).astype(o_ref.dtype)


def _q_proj(x, g, shift, scale, w, gq, mod, tm, out_dtype, q_scale):
    M, K = x.shape
    return pl.pallas_call(
        functools.partial(_q_kernel, q_scale=q_scale),
        grid=(M // tm,),
        in_specs=[
            pl.BlockSpec((tm, K), lambda i: (i, 0)),
            _vec_spec(K),
            mod.spec(tm),
            mod.spec(tm),
            pl.BlockSpec((K, D_MODEL), lambda i: (0, 0)),
            _vec_spec(D_MODEL),
        ],
        out_specs=pl.BlockSpec((tm, D_MODEL), lambda i: (i, 0)),
        out_shape=jax.ShapeDtypeStruct((M, D_MODEL), out_dtype),
        compiler_params=_cparams("parallel"),
        name="q_proj_headnorm",
    )(x, g.reshape(1, K), shift, scale, w, jnp.tile(gq, FOX_HEADS).reshape(1, D_MODEL))


def _kvf_core(x_ref, g_ref, sh_ref, sc_ref, wk_ref, wv_ref, wf_ref, bf_ref, gk_ref):
    h = _normmod(x_ref[...], g_ref[...], sh_ref[...], sc_ref[...]).astype(BF)
    k = _headnorm(_dot(h, wk_ref[...])) * gk_ref[...]
    v = _dot(h, wv_ref[...])
    lf = _log_sigmoid(_dot(h, wf_ref[...]) + bf_ref[...])
    return k, v, lf


def _kvf_rows_kernel(*refs):
    k_ref, v_ref, lf_ref = refs[9:]
    k, v, lf = _kvf_core(*refs[:9])
    k_ref[...] = k
    v_ref[...] = v
    lf_ref[...] = lf[:, :FOX_HEADS]


def _kvf_seq_kernel(*refs, tiles_per_seq):
    kt_ref, vt_ref, lft_ref, cumt_ref, cum_ref, kb_ref, vto_ref, carry_ref = refs[9:]
    tm = kb_ref.shape[0]
    k, v, lf = _kvf_core(*refs[:9])
    kt_ref[...] = k.T
    vt = v.T
    vt_ref[...] = vt
    vto_ref[...] = jnp.ones_like(vto_ref)
    for h in range(FOX_HEADS):
        dst = h * LANES + (h % 2) * FOX_HD
        vto_ref[dst:dst + FOX_HD, :] = vt[h * FOX_HD:(h + 1) * FOX_HD, :].astype(BF)
    kb_ref[...] = k.astype(BF)
    lft_ref[...] = lf.T[:FOX_HEADS, :]

    @pl.when(pl.program_id(0) % tiles_per_seq == 0)
    def _():
        carry_ref[...] = jnp.zeros_like(carry_ref)

    r = lax.broadcasted_iota(jnp.int32, (tm, tm), 0)
    c = lax.broadcasted_iota(jnp.int32, (tm, tm), 1)
    tri = jnp.where(r >= c, 1.0, 0.0)
    cum = _dot_exact(tri, lf) + carry_ref[...]
    carry_ref[...] = cum[tm - 1:tm, :]
    cum_ref[...] = cum[:, :FOX_HEADS]
    cumt_ref[...] = cum.T[:FOX_HEADS, :]


def _kvf_proj(x, g, shift, scale, wk, wv, wf, b_f, gk, mod, tm, seq_len=None):
    M, K = x.shape
    tok = pl.BlockSpec((tm, D_MODEL), lambda i: (i, 0))
    hd = pl.BlockSpec((tm, FOX_HEADS), lambda i: (i, 0))
    w_spec = pl.BlockSpec((K, D_MODEL), lambda i: (0, 0))
    bf_pad = jnp.zeros((1, LANES), F32).at[0, :FOX_HEADS].set(b_f)
    in_specs = [
        pl.BlockSpec((tm, K), lambda i: (i, 0)),
        _vec_spec(K),
        mod.spec(tm),
        mod.spec(tm),
        w_spec,
        w_spec,
        pl.BlockSpec((K, LANES), lambda i: (0, 0)),
        _vec_spec(LANES),
        _vec_spec(D_MODEL),
    ]
    args = (x, g.reshape(1, K), shift, scale, wk, wv, wf, bf_pad, jnp.tile(gk, FOX_HEADS).reshape(1, D_MODEL))
    if seq_len is None:
        return pl.pallas_call(
            _kvf_rows_kernel,
            grid=(M // tm,),
            in_specs=in_specs,
            out_specs=[tok, tok, hd],
            out_shape=[
                jax.ShapeDtypeStruct((M, D_MODEL), F32),
                jax.ShapeDtypeStruct((M, D_MODEL), F32),
                jax.ShapeDtypeStruct((M, FOX_HEADS), F32),
            ],
            compiler_params=_cparams("parallel"),
            name="shared_kvf_rows",
        )(*args)
    tiles = seq_len // tm
    batch = M // seq_len
    feat_t = pl.BlockSpec((None, D_MODEL, tm), lambda i: (i // tiles, 0, i % tiles))
    head_t = pl.BlockSpec((None, FOX_HEADS, tm), lambda i: (i // tiles, 0, i % tiles))
    return pl.pallas_call(
        functools.partial(_kvf_seq_kernel, tiles_per_seq=tiles),
        grid=(M // tm,),
        in_specs=in_specs,
        out_specs=[feat_t, feat_t, head_t, head_t, hd, tok,
                   pl.BlockSpec((None, FOX_HEADS * LANES, tm), lambda i: (i // tiles, 0, i % tiles))],
        out_shape=[
            jax.ShapeDtypeStruct((batch, D_MODEL, seq_len), F32),
            jax.ShapeDtypeStruct((batch, D_MODEL, seq_len), F32),
            jax.ShapeDtypeStruct((batch, FOX_HEADS, seq_len), F32),
            jax.ShapeDtypeStruct((batch, FOX_HEADS, seq_len), F32),
            jax.ShapeDtypeStruct((M, FOX_HEADS), F32),
            jax.ShapeDtypeStruct((M, D_MODEL), BF),
            jax.ShapeDtypeStruct((batch, FOX_HEADS * LANES, seq_len), BF),
        ],
        scratch_shapes=[pltpu.VMEM((1, LANES), F32)],
        compiler_params=_cparams("arbitrary"),
        name="shared_kvf_seq",
    )(*args)


def _out_kernel(a_ref, w_ref, x_ref, gt_ref, o_ref):
    y = _dot(a_ref[...].astype(BF), w_ref[...])
    o_ref[...] = x_ref[...] + (1.0 + gt_ref[...]) * y


def _out_proj(a, w, x, gate, mod, tm):
    M, K = a.shape
    return pl.pallas_call(
        _out_kernel,
        grid=(M // tm,),
        in_specs=[
            pl.BlockSpec((tm, K), lambda i: (i, 0)),
            pl.BlockSpec((K, D_MODEL), lambda i: (0, 0)),
            pl.BlockSpec((tm, D_MODEL), lambda i: (i, 0)),
            mod.spec(tm),
        ],
        out_specs=pl.BlockSpec((tm, D_MODEL), lambda i: (i, 0)),
        out_shape=jax.ShapeDtypeStruct((M, D_MODEL), F32),
        compiler_params=_cparams("parallel"),
        name="out_proj_residual",
    )(a, w, x, gate)


FFN_CHUNK = 256


def _ffn_kernel(x_ref, g_ref, sh_ref, sc_ref, gt_ref, wi_ref, wo_ref, o_ref, h_ref, acc_ref, *, nf):
    h_ref[...] = _normmod(x_ref[...], g_ref[...], sh_ref[...], sc_ref[...]).astype(BF)
    acc_ref[...] = jnp.zeros_like(acc_ref)

    def body(f, carry):
        h = h_ref[...]
        gate = _dot(h, wi_ref[f])
        up = _dot(h, wi_ref[nf + f])
        act = (_silu(gate) * up).astype(BF)
        acc_ref[...] += _dot(act, wo_ref[f])
        return carry

    lax.fori_loop(0, nf, body, 0)
    o_ref[...] = x_ref[...] + (1.0 + gt_ref[...]) * acc_ref[...]


def _ffn(x, g, shift, scale, gate, wi, wo, mod, tm):
    M, K = x.shape
    nf = wo.shape[0]
    resident = pl.Buffered(1)
    return pl.pallas_call(
        functools.partial(_ffn_kernel, nf=nf),
        grid=(M // tm,),
        in_specs=[
            pl.BlockSpec((tm, K), lambda i: (i, 0)),
            _vec_spec(K),
            mod.spec(tm),
            mod.spec(tm),
            mod.spec(tm),
            pl.BlockSpec(wi.shape, lambda i: (0, 0, 0), pipeline_mode=resident),
            pl.BlockSpec(wo.shape, lambda i: (0, 0, 0), pipeline_mode=resident),
        ],
        out_specs=pl.BlockSpec((tm, K), lambda i: (i, 0)),
        out_shape=jax.ShapeDtypeStruct((M, K), F32),
        scratch_shapes=[pltpu.VMEM((tm, K), BF), pltpu.VMEM((tm, K), F32)],
        compiler_params=_cparams("parallel"),
        name="ffn_swiglu",
    )(x, g.reshape(1, K), shift, scale, gate, wi, wo)


def _rotary(x, cos, sin):
    half = x.shape[-1] // 2
    x1, x2 = x[:, :half], x[:, half:]
    return jnp.concatenate([x1 * cos - x2 * sin, x1 * sin + x2 * cos], axis=-1)


def _ret_prompt_kernel(q_ref, k_ref, v_ref, g_ref, cos_ref, sin_ref, lg_ref, o_ref, s_ref, *, rows):
    C = RET_CHUNK

    @pl.when(pl.program_id(2) == 0)
    def _():
        s_ref[...] = jnp.zeros_like(s_ref)

    lg = lg_ref[...]
    i_v = lax.broadcasted_iota(jnp.int32, (C, RET_DV), 0).astype(F32)
    i_k = lax.broadcasted_iota(jnp.int32, (C, RET_DK), 0).astype(F32)
    cross = jnp.exp(lg * (i_v + 1.0))
    kdec = jnp.exp(lg_ref[:, :RET_DK] * (C - 1.0 - i_k))
    cdec = jnp.exp(lg * float(C))
    ii = lax.broadcasted_iota(jnp.int32, (C, C), 0)
    jj = lax.broadcasted_iota(jnp.int32, (C, C), 1)
    diff = (ii - jj).astype(F32)
    inner = jnp.where(diff >= 0, jnp.exp(lg_ref[:, :C] * jnp.maximum(diff, 0.0)), 0.0)

    for c in range(rows // C):
        sl = slice(c * C, (c + 1) * C)
        cos, sin = cos_ref[sl, :], sin_ref[sl, :]
        q = _rotary(q_ref[sl, :].astype(F32), cos, sin)
        k = _rotary(k_ref[sl, :].astype(F32), cos, sin) * (RET_DK ** -0.5)
        v = v_ref[sl, :]
        qb = q.astype(BF)
        S = s_ref[...]
        sc = lax.dot_general(qb, k.astype(BF), _NT, preferred_element_type=F32) * inner
        o = _dot(sc.astype(BF), v) + _dot(qb, S.astype(BF)) * cross
        s_ref[...] = S * cdec + lax.dot_general((k * kdec).astype(BF), v, _TN,
                                                preferred_element_type=F32)
        o = o * lax.rsqrt(jnp.mean(o * o, axis=-1, keepdims=True) + EPS)
        o_ref[sl, :] = (o * _silu(g_ref[sl, :].astype(F32))).astype(o_ref.dtype)


def _ret_prompt(proj, cos, sin, log_g, batch, seq, rows):
    nr = seq // rows
    kq = D_MODEL // RET_DK
    kv = 2 * D_MODEL // RET_DV
    kg = 4 * D_MODEL // RET_DV
    row = lambda b, h, r: b * nr + r
    lg = jnp.broadcast_to(log_g[:, None, None], (RET_HEADS, 1, RET_DV))
    return pl.pallas_call(
        functools.partial(_ret_prompt_kernel, rows=rows),
        grid=(batch, RET_HEADS, nr),
        in_specs=[
            pl.BlockSpec((rows, RET_DK), lambda b, h, r: (row(b, h, r), h)),
            pl.BlockSpec((rows, RET_DK), lambda b, h, r: (row(b, h, r), kq + h)),
            pl.BlockSpec((rows, RET_DV), lambda b, h, r: (row(b, h, r), kv + h)),
            pl.BlockSpec((rows, RET_DV), lambda b, h, r: (row(b, h, r), kg + h)),
            pl.BlockSpec((rows, RET_DK // 2), lambda b, h, r: (r, 0)),
            pl.BlockSpec((rows, RET_DK // 2), lambda b, h, r: (r, 0)),
            pl.BlockSpec((None, 1, RET_DV), lambda b, h, r: (h, 0, 0)),
        ],
        out_specs=[
            pl.BlockSpec((rows, RET_DV), lambda b, h, r: (row(b, h, r), h)),
            pl.BlockSpec((None, None, RET_DK, RET_DV), lambda b, h, r: (b, h, 0, 0)),
        ],
        out_shape=[
            jax.ShapeDtypeStruct((batch * seq, RET_HEADS * RET_DV), BF),
            jax.ShapeDtypeStruct((batch, RET_HEADS, RET_DK, RET_DV), F32),
        ],
        compiler_params=_cparams("parallel", "parallel", "arbitrary"),
        name="retention_prompt",
    )(proj, proj, proj, proj, cos, sin, lg)


def _ret_sample_kernel(p_ref, s0_ref, cos_ref, sin_ref, lg_ref, o_ref, s_ref, *, nb, seq):
    R = nb * seq
    ri = lax.broadcasted_iota(jnp.int32, (R, R), 0)
    rj = lax.broadcasted_iota(jnp.int32, (R, R), 1)
    same = (ri // seq) == (rj // seq)
    diff = (ri - rj).astype(F32)
    t_row = (lax.broadcasted_iota(jnp.int32, (R, 1), 0) % seq).astype(F32)
    b_row = lax.broadcasted_iota(jnp.int32, (R, 1), 0) // seq
    cos, sin = cos_ref[...], sin_ref[...]
    for h in range(RET_HEADS):
        lg = lg_ref[h]
        inner = jnp.where(same & (diff >= 0), jnp.exp(lg[:, :R] * jnp.maximum(diff, 0.0)), 0.0)
        cross = jnp.exp(lg * (t_row + 1.0))
        kdec = jnp.exp(lg[:, :RET_DK] * (seq - 1.0 - t_row))
        cdec = jnp.exp(lg * float(seq))
        q = _rotary(p_ref[:, h * RET_DK:(h + 1) * RET_DK], cos, sin)
        k = _rotary(p_ref[:, D_MODEL + h * RET_DK:D_MODEL + (h + 1) * RET_DK], cos, sin)
        k = k * (RET_DK ** -0.5)
        v = p_ref[:, 2 * D_MODEL + h * RET_DV:2 * D_MODEL + (h + 1) * RET_DV]
        g = p_ref[:, 4 * D_MODEL + h * RET_DV:4 * D_MODEL + (h + 1) * RET_DV]
        qb, vb = q.astype(BF), v.astype(BF)
        sc = lax.dot_general(qb, k.astype(BF), _NT, preferred_element_type=F32) * inner
        o = _dot(sc.astype(BF), vb)
        kd = k * kdec
        for b in range(nb):
            S = s0_ref[b, h]
            mine = b_row == b
            o = o + jnp.where(mine, _dot(qb, S.astype(BF)) * cross, 0.0)
            kb = jnp.where(mine, kd, 0.0).astype(BF)
            s_ref[b, h] = S * cdec + lax.dot_general(kb, vb, _TN, preferred_element_type=F32)
        o = o * lax.rsqrt(jnp.mean(o * o, axis=-1, keepdims=True) + EPS)
        o_ref[:, h * RET_DV:(h + 1) * RET_DV] = o * _silu(g)


def _ret_sample(proj, s0, cos, sin, log_g, seq, nb):
    M = proj.shape[0]
    batch = M // seq
    R = nb * seq
    lg = jnp.broadcast_to(log_g[:, None, None], (RET_HEADS, 1, RET_DV))
    state = pl.BlockSpec((nb, RET_HEADS, RET_DK, RET_DV), lambda i: (i, 0, 0, 0))
    return pl.pallas_call(
        functools.partial(_ret_sample_kernel, nb=nb, seq=seq),
        grid=(batch // nb,),
        in_specs=[
            pl.BlockSpec((R, 6 * D_MODEL), lambda i: (i, 0)),
            state,
            pl.BlockSpec((R, RET_DK // 2), lambda i: (0, 0)),
            pl.BlockSpec((R, RET_DK // 2), lambda i: (0, 0)),
            pl.BlockSpec((RET_HEADS, 1, RET_DV), lambda i: (0, 0, 0)),
        ],
        out_specs=[pl.BlockSpec((R, RET_HEADS * RET_DV), lambda i: (i, 0)), state],
        out_shape=[
            jax.ShapeDtypeStruct((M, RET_HEADS * RET_DV), F32),
            jax.ShapeDtypeStruct(s0.shape, F32),
        ],
        compiler_params=_cparams("parallel"),
        name="retention_sample",
    )(proj, s0, jnp.tile(cos, (nb, 1)), jnp.tile(sin, (nb, 1)), lg)


def _fox_prompt_kernel(qi_ref, kj_ref, q_ref, k_ref, vo_ref, fq_ref, fk_ref, o_ref,
                       qm_ref, m_ref, acc_ref):
    tq = q_ref.shape[0]
    tk = k_ref.shape[0]
    nh = m_ref.shape[0]
    p = pl.program_id(2)
    hg = pl.program_id(1)
    qi = qi_ref[p]
    kj = kj_ref[p]

    def lanes(hh):
        return slice((hh // 2) * LANES, (hh // 2 + 1) * LANES)

    @pl.when(kj == 0)
    def _():
        lane = lax.broadcasted_iota(jnp.int32, (tq, LANES), 1)
        for hh in range(nh):
            q = q_ref[:, lanes(hh)].astype(F32)
            mine = (lane < FOX_HD) if hh % 2 == 0 else (lane >= FOX_HD)
            qm_ref[hh] = jnp.where(mine, q, 0.0).astype(BF)
        m_ref[...] = jnp.full_like(m_ref, NEG)
        acc_ref[...] = jnp.zeros_like(acc_ref)

    def step(diagonal):
        fk = fk_ref[...] * LOG2E
        head = lax.broadcasted_iota(jnp.int32, fk.shape, 1)
        for hh in range(nh):
            k = k_ref[:, lanes(hh)]
            vo = vo_ref[hh * LANES:(hh + 1) * LANES, :]
            fk_col = jnp.sum(jnp.where(head == nh * hg + hh, fk, 0.0), axis=1, keepdims=True)
            st = lax.dot_general(k, qm_ref[hh], _NT, preferred_element_type=F32) - fk_col
            if diagonal:
                krow = lax.broadcasted_iota(jnp.int32, (tk, tq), 0)
                qcol = lax.broadcasted_iota(jnp.int32, (tk, tq), 1)
                st = jnp.where(krow <= qcol, st, NEG)
            fq = fq_ref[hh:hh + 1, :] * LOG2E
            m_old = m_ref[hh]
            m_new = jnp.maximum(m_old, jnp.max(st, axis=0, keepdims=True) + fq)
            alpha = jnp.exp2(m_old - m_new)
            pm = jnp.exp2(st + (fq - m_new))
            acc_ref[hh] = alpha * acc_ref[hh] + _dot(vo, pm.astype(BF))
            m_ref[hh] = m_new

    @pl.when(kj < qi)
    def _():
        step(False)

    @pl.when(kj == qi)
    def _():
        step(True)
        for hh in range(0, nh, 2):
            even, odd = acc_ref[hh], acc_ref[hh + 1]
            ot = jnp.concatenate([even[:FOX_HD] / even[FOX_HD:FOX_HD + 1],
                                  odd[FOX_HD:] / odd[0:1]], axis=0)
            o_ref[:, lanes(hh)] = ot.T.astype(o_ref.dtype)


def _fox_prompt(q, k, vo, cum, cum_t, batch, seq, tq, nh):
    nq = seq // tq
    pairs = [(i, j) for i in range(nq) for j in range(i + 1)]
    qi_tab = jnp.asarray([p[0] for p in pairs], jnp.int32)
    kj_tab = jnp.asarray([p[1] for p in pairs], jnp.int32)
    groups = FOX_HEADS // nh
    w = nh * FOX_HD
    f_key = cum.reshape(batch, seq, FOX_HEADS)
    f_query = cum_t.reshape(batch, groups, nh, seq)
    grid_spec = pltpu.PrefetchScalarGridSpec(
        num_scalar_prefetch=2,
        grid=(batch, groups, len(pairs)),
        in_specs=[
            pl.BlockSpec((tq, w), lambda b, h, p, qi, kj: (b * nq + qi[p], h)),
            pl.BlockSpec((tq, w), lambda b, h, p, qi, kj: (b * nq + kj[p], h)),
            pl.BlockSpec((None, nh * LANES, tq), lambda b, h, p, qi, kj: (b, h, kj[p])),
            pl.BlockSpec((None, None, nh, tq), lambda b, h, p, qi, kj: (b, h, 0, qi[p])),
            pl.BlockSpec((None, tq, FOX_HEADS), lambda b, h, p, qi, kj: (b, kj[p], 0)),
        ],
        out_specs=pl.BlockSpec((tq, w), lambda b, h, p, qi, kj: (b * nq + qi[p], h)),
        scratch_shapes=[
            pltpu.VMEM((nh, tq, LANES), BF),
            pltpu.VMEM((nh, 1, tq), F32),
            pltpu.VMEM((nh, LANES, tq), F32),
        ],
    )
    return pl.pallas_call(
        _fox_prompt_kernel,
        grid_spec=grid_spec,
        out_shape=jax.ShapeDtypeStruct((batch * seq, D_MODEL), BF),
        compiler_params=_cparams("parallel", "parallel", "arbitrary"),
        name="fox_prompt_attention",
    )(qi_tab, kj_tab, q, k, vo, f_query, f_key)


def _fox_sample_kernel(pt_ref, q_ref, kn_ref, vn_ref, lfn_ref, *refs, npg, seq):
    kt_refs = refs[:npg]
    vt_refs = refs[npg:2 * npg]
    lf_refs = refs[2 * npg:3 * npg]
    o_ref = refs[3 * npg]
    qbd_ref, carry_ref, fn_ref, m_ref, l_ref, acc_ref, kpad_ref, vpad_ref = refs[3 * npg + 1:]
    g = pl.program_id(1)
    C = seq * FOX_HEADS
    P = kpad_ref.shape[0]
    row = lax.broadcasted_iota(jnp.int32, (C, D_MODEL), 0)
    col = lax.broadcasted_iota(jnp.int32, (C, D_MODEL), 1)
    own_head = (col >> 6) == (row % FOX_HEADS)
    t_of = lax.broadcasted_iota(jnp.int32, (C, P), 0) // FOX_HEADS
    j_of = lax.broadcasted_iota(jnp.int32, (C, P), 1)
    kk = lax.broadcasted_iota(jnp.int32, (P, P), 0)
    jj = lax.broadcasted_iota(jnp.int32, (P, P), 1)

    def tile_heads(a):
        return jnp.concatenate([a] * seq, axis=0)

    @pl.when(g == 0)
    def _():
        q = q_ref[...]
        qrep = jnp.concatenate(
            [jnp.broadcast_to(q[t:t + 1, :], (FOX_HEADS, D_MODEL)) for t in range(seq)], axis=0)
        qbd = jnp.where(own_head, qrep, 0.0).astype(BF)
        qbd_ref[...] = qbd
        kpad_ref[...] = jnp.zeros_like(kpad_ref)
        vpad_ref[...] = jnp.zeros_like(vpad_ref)
        kpad_ref[0:seq, :] = kn_ref[...]
        vpad_ref[0:seq, :] = vn_ref[...]
        fn = _dot_exact(lfn_ref[...], jnp.where(kk <= jj, 1.0, 0.0))
        fn_c = tile_heads(fn)
        fn_col = jnp.sum(jnp.where(j_of == t_of, fn_c, 0.0), axis=-1, keepdims=True)
        fn_ref[...] = fn_col
        s = lax.dot_general(qbd, kpad_ref[...].astype(BF), _NT, preferred_element_type=F32)
        s = jnp.where(j_of <= t_of, s + fn_col - fn_c, NEG)
        m = jnp.max(s, axis=-1, keepdims=True)
        pm = jnp.exp(s - m)
        m_ref[...] = m
        l_ref[...] = jnp.sum(pm, axis=-1, keepdims=True)
        acc_ref[...] = _dot(pm.astype(BF), vpad_ref[...].astype(BF))
        carry_ref[...] = jnp.zeros_like(carry_ref)

    later = jnp.where(kk > jj, 1.0, 0.0)
    qbd = qbd_ref[...]
    fn_col = fn_ref[...]
    carry = carry_ref[...]
    scores = []
    for i in reversed(range(npg)):
        lf = lf_refs[i][...]
        r_page = _dot_exact(lf, later) + carry
        carry = carry + jnp.sum(lf, axis=-1, keepdims=True)
        s = _dot(qbd, kt_refs[i][...].astype(BF))
        scores.append((i, s + fn_col + tile_heads(r_page)))
    carry_ref[...] = carry
    m_old = m_ref[...]
    m_new = m_old
    for _, s in scores:
        m_new = jnp.maximum(m_new, jnp.max(s, axis=-1, keepdims=True))
    alpha = jnp.exp(m_old - m_new)
    l_new = alpha * l_ref[...]
    acc = alpha * acc_ref[...]
    for i, s in scores:
        pm = jnp.exp(s - m_new)
        l_new = l_new + jnp.sum(pm, axis=-1, keepdims=True)
        acc = acc + lax.dot_general(pm.astype(BF), vt_refs[i][...].astype(BF), _NT,
                                    preferred_element_type=F32)
    m_ref[...] = m_new
    l_ref[...] = l_new
    acc_ref[...] = acc

    @pl.when(g == pl.num_programs(1) - 1)
    def _():
        on = jnp.where(own_head, acc / l_new, 0.0)
        for t in range(seq):
            o_ref[t:t + 1, :] = jnp.sum(on[t * FOX_HEADS:(t + 1) * FOX_HEADS, :], axis=0, keepdims=True)


def _fox_sample(q, k_new, v_new, lf_new_t, cache_kt, cache_vt, cache_lf_t, page_table, seq, npg):
    batch, n_pages = page_table.shape
    P = cache_kt.shape[-1]
    groups = n_pages // npg
    C = seq * FOX_HEADS

    def page(i):
        return lambda b, g, pt: (pt[b, (groups - 1 - g) * npg + i], 0, 0)

    new_tok = pl.BlockSpec((None, seq, D_MODEL), lambda b, g, pt: (b, 0, 0))
    grid_spec = pltpu.PrefetchScalarGridSpec(
        num_scalar_prefetch=1,
        grid=(batch, groups),
        in_specs=(
            [new_tok, new_tok, new_tok,
             pl.BlockSpec((None, FOX_HEADS, P), lambda b, g, pt: (b, 0, 0))]
            + [pl.BlockSpec((None, D_MODEL, P), page(i)) for i in range(npg)]
            + [pl.BlockSpec((None, D_MODEL, P), page(i)) for i in range(npg)]
            + [pl.BlockSpec((None, FOX_HEADS, P), page(i)) for i in range(npg)]
        ),
        out_specs=new_tok,
        scratch_shapes=[
            pltpu.VMEM((C, D_MODEL), BF),
            pltpu.VMEM((FOX_HEADS, 1), F32),
            pltpu.VMEM((C, 1), F32),
            pltpu.VMEM((C, 1), F32),
            pltpu.VMEM((C, 1), F32),
            pltpu.VMEM((C, D_MODEL), F32),
            pltpu.VMEM((P, D_MODEL), F32),
            pltpu.VMEM((P, D_MODEL), F32),
        ],
    )
    return pl.pallas_call(
        functools.partial(_fox_sample_kernel, npg=npg, seq=seq),
        grid_spec=grid_spec,
        out_shape=jax.ShapeDtypeStruct((batch, seq, D_MODEL), F32),
        compiler_params=_cparams("parallel", "arbitrary"),
        name="fox_sample_attention",
    )(page_table, q, k_new, v_new, lf_new_t,
      *([cache_kt] * npg), *([cache_vt] * npg), *([cache_lf_t] * npg))


def _prep_weights(w_ret_in, w_ret_out, w_q, w_o, w_ffn_in, w_ffn_out, w_kvf):
    depth, _, two_f = w_ffn_in.shape
    d_ff = two_f // 2
    nf = d_ff // FFN_CHUNK
    wi = w_ffn_in.astype(BF).reshape(depth, D_MODEL, 2 * nf, FFN_CHUNK).transpose(0, 2, 1, 3)
    wo = w_ffn_out.astype(BF).reshape(depth, nf, FFN_CHUNK, D_MODEL)
    w_kvf_b = w_kvf.astype(BF)
    wf = jnp.zeros((D_MODEL, LANES), BF).at[:, :FOX_HEADS].set(w_kvf_b[:, 2 * D_MODEL:])
    return dict(
        ret_in=w_ret_in.astype(BF), ret_out=w_ret_out.astype(BF), q=w_q.astype(BF), o=w_o.astype(BF),
        ffn_in=wi, ffn_out=wo, wk=w_kvf_b[:, :D_MODEL], wv=w_kvf_b[:, D_MODEL:2 * D_MODEL], wf=wf)


def _rope_tables(pos):
    half = RET_DK // 2
    freqs = ROPE_BASE ** (-jnp.arange(half, dtype=F32) / half)
    ang = pos.astype(F32)[:, None] * freqs[None, :]
    return jnp.cos(ang), jnp.sin(ang)


def _trunk(x, mods, kvmod, mod, pos, W, norms, b_f, *, tm, kvf_seq, ret_fn, attn_fn):
    ret_norm_g, attn_norm_g, q_norm_g, ffn_norm_g, kv_norm_g, k_norm_g = norms
    n_a = W["ret_in"].shape[0]
    depth = W["ffn_in"].shape[0]
    log_g = jnp.log1p(-jnp.exp2(-5.0 - jnp.arange(RET_HEADS, dtype=F32)))
    cos, sin = _rope_tables(pos)
    states = []
    shared = None
    for l in range(depth):
        sh1, sc1, g1, sh2, sc2, g2 = mods[l]
        if l < n_a:
            a, S = ret_fn(l, x, ret_norm_g[l], sh1, sc1, W["ret_in"][l], cos, sin, log_g)
            states.append(S)
            x = _out_proj(a, W["ret_out"][l], x, g1, mod, tm)
        else:
            j = l - n_a
            if shared is None:
                shared = _kvf_proj(x, kv_norm_g, kvmod[0], kvmod[1], W["wk"], W["wv"], W["wf"],
                                   b_f, k_norm_g, mod, min(tm, 512), kvf_seq)
            a = attn_fn(j, x, attn_norm_g[j], sh1, sc1, W["q"][j], q_norm_g[j], shared)
            x = _out_proj(a, W["o"][j], x, g1, mod, tm)
        x = _ffn(x, ffn_norm_g[l], sh2, sc2, g2, W["ffn_in"][l], W["ffn_out"][l], mod, tm)
    return x, jnp.stack(states), shared


def kernel(x_prompt, x_sample, cache_k, cache_v, cache_logf, state_ret, page_table, c_prompt, c_sample,
           w_mod, b_mod, ret_norm_g, w_ret_in, w_ret_out, attn_norm_g, w_q, q_norm_g, w_o, ffn_norm_g,
           w_ffn_in, w_ffn_out, kv_norm_g, w_kvmod, b_kvmod, w_kvf, b_f, k_norm_g):
    Bp, Tp, _ = x_prompt.shape
    Bd, Td, _ = x_sample.shape
    depth = w_mod.shape[0]
    n_pool, page_size = cache_k.shape[:2]
    past_len = page_table.shape[1] * page_size
    W = _prep_weights(w_ret_in, w_ret_out, w_q, w_o, w_ffn_in, w_ffn_out, w_kvf)
    norms = (ret_norm_g, attn_norm_g, q_norm_g, ffn_norm_g, kv_norm_g, k_norm_g)

    c_all = jnp.concatenate([c_prompt, c_sample], axis=0)
    mod_all = _cond_proj(c_all, w_mod, b_mod, 1536)
    kvmod_all = _cond_proj(c_all, w_kvmod[None], b_kvmod[None], 1024)[0]

    def split_mods(lo, hi, expand):
        def lay(a):
            return jnp.repeat(a, expand, axis=0)[None] if expand else a[:, None, :]
        mods = [[lay(m) for m in jnp.split(mod_all[l, lo:hi], 6, axis=-1)] for l in range(depth)]
        kvmod = [lay(m) for m in jnp.split(kvmod_all[lo:hi], 2, axis=-1)]
        return mods, kvmod

    mods_p, kvmod_p = split_mods(0, Bp, 0)
    mod_p = _Mod(Tp, per_row=False)
    tm_p = 1024

    def ret_prompt(l, x, g, sh, sc, w, cos, sin, log_g):
        proj = _nm_matmul(x, g, sh, sc, w, mod_p, tm_p, 3072, BF)
        return _ret_prompt(proj, cos, sin, log_g, Bp, Tp, 1024 if l == 0 else 2048)

    def attn_prompt(j, x, g, sh, sc, w, gq, shared):
        _, _, _, cum_t, cum, kb, vo = shared
        q = _q_proj(x, g, sh, sc, w, gq, mod_p, tm_p, BF, FOX_HD ** -0.5 * LOG2E)
        return _fox_prompt(q, kb, vo, cum, cum_t, Bp, Tp, 1024, 4 if j == 0 else 8)

    y_p, st_p, (kt_p, vt_p, lft_p, _, _, _, _) = _trunk(
        x_prompt.reshape(Bp * Tp, D_MODEL), mods_p, kvmod_p, mod_p, jnp.arange(Tp), W, norms, b_f,
        tm=tm_p, kvf_seq=Tp, ret_fn=ret_prompt, attn_fn=attn_prompt)

    mods_s, kvmod_s = split_mods(Bp, Bp + Bd, Td)
    Ms = Bd * Td
    mod_s = _Mod(Ms, per_row=True)
    ckt = cache_k.transpose(0, 2, 3, 1).reshape(n_pool, D_MODEL, page_size)
    cvt = cache_v.transpose(0, 2, 3, 1).reshape(n_pool, D_MODEL, page_size)
    clf_t = cache_logf.transpose(0, 2, 1)

    def ret_sample(l, x, g, sh, sc, w, cos, sin, log_g):
        proj = _nm_matmul(x, g, sh, sc, w, mod_s, Ms, 1536, F32)
        return _ret_sample(proj, state_ret[l], cos, sin, log_g, Td, 2)

    def attn_sample(j, x, g, sh, sc, w, gq, shared):
        k, v, lf = shared
        q = _q_proj(x, g, sh, sc, w, gq, mod_s, Ms, F32, FOX_HD ** -0.5)
        lf_t = jnp.pad(lf.reshape(Bd, Td, FOX_HEADS).transpose(0, 2, 1),
                       ((0, 0), (0, 0), (0, page_size - Td)))
        o = _fox_sample(q.reshape(Bd, Td, D_MODEL), k.reshape(Bd, Td, D_MODEL), v.reshape(Bd, Td, D_MODEL),
                        lf_t, ckt, cvt, clf_t, page_table, Td, 8)
        return o.reshape(Ms, D_MODEL)

    y_s, st_s, (k_s, v_s, lf_s) = _trunk(
        x_sample.reshape(Ms, D_MODEL), mods_s, kvmod_s, mod_s, past_len + jnp.arange(Td), W, norms, b_f,
        tm=Ms, kvf_seq=None, ret_fn=ret_sample, attn_fn=attn_sample)

    heads = (FOX_HEADS, FOX_HD)
    k_p = kt_p.reshape(Bp, *heads, Tp).transpose(0, 3, 1, 2)
    v_p = vt_p.reshape(Bp, *heads, Tp).transpose(0, 3, 1, 2)
    return (y_p.reshape(Bp, Tp, D_MODEL), y_s.reshape(Bd, Td, D_MODEL), st_p, st_s,
            k_p, v_p, lft_p.transpose(0, 2, 1),
            k_s.reshape(Bd, Td, *heads), v_s.reshape(Bd, Td, *heads), lf_s.reshape(Bd, Td, FOX_HEADS))
```

```python
import functools
import math

import jax
import jax.numpy as jnp
from jax import lax
from jax.experimental import pallas as pl
from jax.experimental.pallas import tpu as pltpu

BF = jnp.bfloat16
F32 = jnp.float32

D_MODEL = 1024
RET_HEADS = 4
RET_DK = 256
RET_DV = 512
RET_CHUNK = 128
ROPE_BASE = 10000.0
FOX_HEADS = 16
FOX_HD = 64
EPS = 1e-6
LANES = 128
VMEM_LIMIT = 52 * 1024 * 1024
NEG = -1e30
LOG2E = math.log2(math.e)

_NT = (((1,), (1,)), ((), ()))
_TN = (((0,), (0,)), ((), ()))


def _cparams(*sem):
    return pltpu.CompilerParams(dimension_semantics=sem, vmem_limit_bytes=VMEM_LIMIT)


def _dot(a, b):
    return jnp.dot(a, b, preferred_element_type=F32)


def _dot_exact(a, b):
    return jnp.dot(a, b, preferred_element_type=F32, precision=lax.Precision.HIGHEST)


def _silu(x):
    return x * jax.nn.sigmoid(x)


def _normmod(x, g, shift, scale):
    ms = jnp.mean(x * x, axis=-1, keepdims=True)
    y = x * lax.rsqrt(ms + EPS) * g
    return y * (1.0 + scale) + shift


def _split_bf16(x):
    hi = x.astype(BF)
    lo = (x - hi.astype(F32)).astype(BF)
    return hi, lo


def _headnorm(a):
    r = lax.broadcasted_iota(jnp.int32, (D_MODEL, LANES), 0) >> 6
    c = lax.broadcasted_iota(jnp.int32, (D_MODEL, LANES), 1)
    seg = jnp.where(r == c, 1.0, 0.0).astype(BF)
    rt = lax.broadcasted_iota(jnp.int32, (LANES, D_MODEL), 0)
    ct = lax.broadcasted_iota(jnp.int32, (LANES, D_MODEL), 1) >> 6
    seg_t = jnp.where(rt == ct, 1.0, 0.0).astype(BF)
    y_hi, y_lo = _split_bf16(a * a)
    ms = (_dot(y_hi, seg) + _dot(y_lo, seg)) * (1.0 / FOX_HD)
    rs = lax.rsqrt(ms + EPS)
    r_hi, r_lo = _split_bf16(rs)
    return a * (_dot(r_hi, seg_t) + _dot(r_lo, seg_t))


def _log_sigmoid(x):
    return jnp.minimum(x, 0.0) - jnp.log1p(jnp.exp(-jnp.abs(x)))


def _cond_kernel(c_ref, w_ref, b_ref, o_ref):
    ca = _silu(c_ref[...]).astype(BF)
    o_ref[...] = _dot(ca, w_ref[...].astype(BF)) + b_ref[...]


def _cond_proj(c, w, b, tn):
    L, K, N = w.shape
    R = c.shape[0]
    return pl.pallas_call(
        _cond_kernel,
        grid=(L, N // tn),
        in_specs=[
            pl.BlockSpec((R, K), lambda l, j: (0, 0)),
            pl.BlockSpec((None, K, tn), lambda l, j: (l, 0, j)),
            pl.BlockSpec((None, 1, tn), lambda l, j: (l, 0, j)),
        ],
        out_specs=pl.BlockSpec((None, R, tn), lambda l, j: (l, 0, j)),
        out_shape=jax.ShapeDtypeStruct((L, R, N), F32),
        compiler_params=_cparams("parallel", "parallel"),
        name="cond_proj",
    )(c, w, b.reshape(L, 1, N))


class _Mod:
    def __init__(self, rows_per_group, per_row):
        self.rows_per_group = rows_per_group
        self.per_row = per_row

    def spec(self, tm):
        if self.per_row:
            return pl.BlockSpec((None, tm, D_MODEL), lambda i, *_: (0, i, 0))
        tiles = self.rows_per_group // tm
        return pl.BlockSpec((None, 1, D_MODEL), lambda i, *_: (i // tiles, 0, 0))


def _vec_spec(n):
    return pl.BlockSpec((1, n), lambda i, *_: (0, 0))


def _nm_kernel(x_ref, g_ref, sh_ref, sc_ref, w_ref, o_ref, h_ref):
    @pl.when(pl.program_id(1) == 0)
    def _():
        h_ref[...] = _normmod(x_ref[...], g_ref[...], sh_ref[...], sc_ref[...]).astype(BF)

    o_ref[...] = _dot(h_ref[...], w_ref[...]).astype(o_ref.dtype)


def _nm_matmul(x, g, shift, scale, w, layer, mod, tm, tn, out_dtype):
    M, K = x.shape
    N = w.shape[2]
    return pl.pallas_call(
        _nm_kernel,
        grid=(M // tm, N // tn),
        in_specs=[
            pl.BlockSpec((tm, K), lambda i, j: (i, 0)),
            _vec_spec(K),
            mod.spec(tm),
            mod.spec(tm),
            pl.BlockSpec((None, K, tn), lambda i, j: (layer, 0, j)),
        ],
        out_specs=pl.BlockSpec((tm, tn), lambda i, j: (i, j)),
        out_shape=jax.ShapeDtypeStruct((M, N), out_dtype),
        scratch_shapes=[pltpu.VMEM((tm, K), BF)],
        compiler_params=_cparams("parallel", "arbitrary"),
        name="norm_mod_matmul",
    )(x, g.reshape(1, K), shift, scale, w)


def _q_kernel(x_ref, g_ref, sh_ref, sc_ref, w_ref, gq_ref, o_ref, *, q_scale):
    h = _normmod(x_ref[...], g_ref[...], sh_ref[...], sc_ref[...]).astype(BF)
    q = _headnorm(_dot(h, w_ref[...])) * gq_ref[...]
    o_ref[...] = (q * q_scale).astype(o_ref.dtype)


def _q_proj(x, g, shift, scale, w, layer, gq, mod, tm, out_dtype, q_scale):
    M, K = x.shape
    return pl.pallas_call(
        functools.partial(_q_kernel, q_scale=q_scale),
        grid=(M // tm,),
        in_specs=[
            pl.BlockSpec((tm, K), lambda i: (i, 0)),
            _vec_spec(K),
            mod.spec(tm),
            mod.spec(tm),
            pl.BlockSpec((None, K, D_MODEL), lambda i: (layer, 0, 0)),
            _vec_spec(D_MODEL),
        ],
        out_specs=pl.BlockSpec((tm, D_MODEL), lambda i: (i, 0)),
        out_shape=jax.ShapeDtypeStruct((M, D_MODEL), out_dtype),
        compiler_params=_cparams("parallel"),
        name="q_proj_headnorm",
    )(x, g.reshape(1, K), shift, scale, w, jnp.tile(gq, FOX_HEADS).reshape(1, D_MODEL))


def _kvf_core(x_ref, g_ref, sh_ref, sc_ref, wk_ref, wv_ref, wf_ref, bf_ref, gk_ref):
    h = _normmod(x_ref[...], g_ref[...], sh_ref[...], sc_ref[...]).astype(BF)
    k = _headnorm(_dot(h, wk_ref[...])) * gk_ref[...]
    v = _dot(h, wv_ref[...])
    lf = _log_sigmoid(_dot(h, wf_ref[...]) + bf_ref[...])
    return k, v, lf


def _kvf_rows_kernel(*refs):
    k_ref, v_ref, lf_ref = refs[9:]
    k, v, lf = _kvf_core(*refs[:9])
    k_ref[...] = k
    v_ref[...] = v
    lf_ref[...] = lf[:, :FOX_HEADS]


def _kvf_seq_kernel(*refs, tiles_per_seq):
    kt_ref, vt_ref, lft_ref, cumt_ref, cum_ref, kb_ref, vto_ref, carry_ref = refs[9:]
    tm = kb_ref.shape[0]
    k, v, lf = _kvf_core(*refs[:9])
    kt_ref[...] = k.T
    vt = v.T
    vt_ref[...] = vt
    vto_ref[...] = jnp.ones_like(vto_ref)
    for h in range(FOX_HEADS):
        dst = h * LANES + (h % 2) * FOX_HD
        vto_ref[dst:dst + FOX_HD, :] = vt[h * FOX_HD:(h + 1) * FOX_HD, :].astype(BF)
    kb_ref[...] = k.astype(BF)
    lft_ref[...] = lf.T[:FOX_HEADS, :]

    @pl.when(pl.program_id(0) % tiles_per_seq == 0)
    def _():
        carry_ref[...] = jnp.zeros_like(carry_ref)

    r = lax.broadcasted_iota(jnp.int32, (tm, tm), 0)
    c = lax.broadcasted_iota(jnp.int32, (tm, tm), 1)
    tri = jnp.where(r >= c, 1.0, 0.0)
    cum = _dot_exact(tri, lf) + carry_ref[...]
    carry_ref[...] = cum[tm - 1:tm, :]
    cum_ref[...] = cum[:, :FOX_HEADS]
    cumt_ref[...] = cum.T[:FOX_HEADS, :]


def _kvf_proj(x, g, shift, scale, wk, wv, wf, b_f, gk, mod, tm, seq_len=None):
    M, K = x.shape
    tok = pl.BlockSpec((tm, D_MODEL), lambda i: (i, 0))
    hd = pl.BlockSpec((tm, FOX_HEADS), lambda i: (i, 0))
    w_spec = pl.BlockSpec((K, D_MODEL), lambda i: (0, 0))
    bf_pad = jnp.zeros((1, LANES), F32).at[0, :FOX_HEADS].set(b_f)
    in_specs = [
        pl.BlockSpec((tm, K), lambda i: (i, 0)),
        _vec_spec(K),
        mod.spec(tm),
        mod.spec(tm),
        w_spec,
        w_spec,
        pl.BlockSpec((K, LANES), lambda i: (0, 0)),
        _vec_spec(LANES),
        _vec_spec(D_MODEL),
    ]
    args = (x, g.reshape(1, K), shift, scale, wk, wv, wf, bf_pad, jnp.tile(gk, FOX_HEADS).reshape(1, D_MODEL))
    if seq_len is None:
        return pl.pallas_call(
            _kvf_rows_kernel,
            grid=(M // tm,),
            in_specs=in_specs,
            out_specs=[tok, tok, hd],
            out_shape=[
                jax.ShapeDtypeStruct((M, D_MODEL), F32),
                jax.ShapeDtypeStruct((M, D_MODEL), F32),
                jax.ShapeDtypeStruct((M, FOX_HEADS), F32),
            ],
            compiler_params=_cparams("parallel"),
            name="shared_kvf_rows",
        )(*args)
    tiles = seq_len // tm
    batch = M // seq_len
    feat_t = pl.BlockSpec((None, D_MODEL, tm), lambda i: (i // tiles, 0, i % tiles))
    head_t = pl.BlockSpec((None, FOX_HEADS, tm), lambda i: (i // tiles, 0, i % tiles))
    return pl.pallas_call(
        functools.partial(_kvf_seq_kernel, tiles_per_seq=tiles),
        grid=(M // tm,),
        in_specs=in_specs,
        out_specs=[feat_t, feat_t, head_t, head_t, hd, tok,
                   pl.BlockSpec((None, FOX_HEADS * LANES, tm), lambda i: (i // tiles, 0, i % tiles))],
        out_shape=[
            jax.ShapeDtypeStruct((batch, D_MODEL, seq_len), F32),
            jax.ShapeDtypeStruct((batch, D_MODEL, seq_len), F32),
            jax.ShapeDtypeStruct((batch, FOX_HEADS, seq_len), F32),
            jax.ShapeDtypeStruct((batch, FOX_HEADS, seq_len), F32),
            jax.ShapeDtypeStruct((M, FOX_HEADS), F32),
            jax.ShapeDtypeStruct((M, D_MODEL), BF),
            jax.ShapeDtypeStruct((batch, FOX_HEADS * LANES, seq_len), BF),
        ],
        scratch_shapes=[pltpu.VMEM((1, LANES), F32)],
        compiler_params=_cparams("arbitrary"),
        name="shared_kvf_seq",
    )(*args)


def _out_kernel(a_ref, w_ref, x_ref, gt_ref, o_ref):
    y = _dot(a_ref[...].astype(BF), w_ref[...])
    o_ref[...] = x_ref[...] + (1.0 + gt_ref[...]) * y


def _out_proj(a, w, layer, x, gate, mod, tm):
    M, K = a.shape
    return pl.pallas_call(
        _out_kernel,
        grid=(M // tm,),
        in_specs=[
            pl.BlockSpec((tm, K), lambda i: (i, 0)),
            pl.BlockSpec((None, K, D_MODEL), lambda i: (layer, 0, 0)),
            pl.BlockSpec((tm, D_MODEL), lambda i: (i, 0)),
            mod.spec(tm),
        ],
        out_specs=pl.BlockSpec((tm, D_MODEL), lambda i: (i, 0)),
        out_shape=jax.ShapeDtypeStruct((M, D_MODEL), F32),
        compiler_params=_cparams("parallel"),
        name="out_proj_residual",
    )(a, w, x, gate)


FFN_CHUNK = 256


def _ffn_kernel(x_ref, g_ref, sh_ref, sc_ref, gt_ref, wi_ref, wo_ref, o_ref, h_ref, acc_ref, *, nf):
    h_ref[...] = _normmod(x_ref[...], g_ref[...], sh_ref[...], sc_ref[...]).astype(BF)
    acc_ref[...] = jnp.zeros_like(acc_ref)

    def body(f, carry):
        h = h_ref[...]
        gate = _dot(h, wi_ref[f])
        up = _dot(h, wi_ref[nf + f])
        act = (_silu(gate) * up).astype(BF)
        acc_ref[...] += _dot(act, wo_ref[f])
        return carry

    lax.fori_loop(0, nf, body, 0)
    o_ref[...] = x_ref[...] + (1.0 + gt_ref[...]) * acc_ref[...]


def _ffn(x, g, shift, scale, gate, wi, wo, layer, mod, tm):
    M, K = x.shape
    nf = wo.shape[1]
    resident = pl.Buffered(1)
    return pl.pallas_call(
        functools.partial(_ffn_kernel, nf=nf),
        grid=(M // tm,),
        in_specs=[
            pl.BlockSpec((tm, K), lambda i: (i, 0)),
            _vec_spec(K),
            mod.spec(tm),
            mod.spec(tm),
            mod.spec(tm),
            pl.BlockSpec((None,) + wi.shape[1:], lambda i: (layer, 0, 0, 0), pipeline_mode=resident),
            pl.BlockSpec((None,) + wo.shape[1:], lambda i: (layer, 0, 0, 0), pipeline_mode=resident),
        ],
        out_specs=pl.BlockSpec((tm, K), lambda i: (i, 0)),
        out_shape=jax.ShapeDtypeStruct((M, K), F32),
        scratch_shapes=[pltpu.VMEM((tm, K), BF), pltpu.VMEM((tm, K), F32)],
        compiler_params=_cparams("parallel"),
        name="ffn_swiglu",
    )(x, g.reshape(1, K), shift, scale, gate, wi, wo)


def _rotary(x, cos, sin):
    half = x.shape[-1] // 2
    x1, x2 = x[:, :half], x[:, half:]
    return jnp.concatenate([x1 * cos - x2 * sin, x1 * sin + x2 * cos], axis=-1)


def _ret_prompt_kernel(q_ref, k_ref, v_ref, g_ref, cos_ref, sin_ref, lg_ref, o_ref, s_ref, *, rows):
    C = RET_CHUNK

    @pl.when(pl.program_id(2) == 0)
    def _():
        s_ref[...] = jnp.zeros_like(s_ref)

    lg = lg_ref[...]
    i_v = lax.broadcasted_iota(jnp.int32, (C, RET_DV), 0).astype(F32)
    i_k = lax.broadcasted_iota(jnp.int32, (C, RET_DK), 0).astype(F32)
    cross = jnp.exp(lg * (i_v + 1.0))
    kdec = jnp.exp(lg_ref[:, :RET_DK] * (C - 1.0 - i_k))
    cdec = jnp.exp(lg * float(C))
    ii = lax.broadcasted_iota(jnp.int32, (C, C), 0)
    jj = lax.broadcasted_iota(jnp.int32, (C, C), 1)
    diff = (ii - jj).astype(F32)
    inner = jnp.where(diff >= 0, jnp.exp(lg_ref[:, :C] * jnp.maximum(diff, 0.0)), 0.0)

    for c in range(rows // C):
        sl = slice(c * C, (c + 1) * C)
        cos, sin = cos_ref[sl, :], sin_ref[sl, :]
        q = _rotary(q_ref[sl, :].astype(F32), cos, sin)
        k = _rotary(k_ref[sl, :].astype(F32), cos, sin) * (RET_DK ** -0.5)
        v = v_ref[sl, :]
        qb = q.astype(BF)
        S = s_ref[...]
        sc = lax.dot_general(qb, k.astype(BF), _NT, preferred_element_type=F32) * inner
        o = _dot(sc.astype(BF), v) + _dot(qb, S.astype(BF)) * cross
        s_ref[...] = S * cdec + lax.dot_general((k * kdec).astype(BF), v, _TN,
                                                preferred_element_type=F32)
        o = o * lax.rsqrt(jnp.mean(o * o, axis=-1, keepdims=True) + EPS)
        o_ref[sl, :] = (o * _silu(g_ref[sl, :].astype(F32))).astype(o_ref.dtype)


def _ret_prompt(proj, cos, sin, log_g, batch, seq, rows):
    nr = seq // rows
    kq = D_MODEL // RET_DK
    kv = 2 * D_MODEL // RET_DV
    kg = 4 * D_MODEL // RET_DV
    row = lambda b, h, r: b * nr + r
    lg = jnp.broadcast_to(log_g[:, None, None], (RET_HEADS, 1, RET_DV))
    return pl.pallas_call(
        functools.partial(_ret_prompt_kernel, rows=rows),
        grid=(batch, RET_HEADS, nr),
        in_specs=[
            pl.BlockSpec((rows, RET_DK), lambda b, h, r: (row(b, h, r), h)),
            pl.BlockSpec((rows, RET_DK), lambda b, h, r: (row(b, h, r), kq + h)),
            pl.BlockSpec((rows, RET_DV), lambda b, h, r: (row(b, h, r), kv + h)),
            pl.BlockSpec((rows, RET_DV), lambda b, h, r: (row(b, h, r), kg + h)),
            pl.BlockSpec((rows, RET_DK // 2), lambda b, h, r: (r, 0)),
            pl.BlockSpec((rows, RET_DK // 2), lambda b, h, r: (r, 0)),
            pl.BlockSpec((None, 1, RET_DV), lambda b, h, r: (h, 0, 0)),
        ],
        out_specs=[
            pl.BlockSpec((rows, RET_DV), lambda b, h, r: (row(b, h, r), h)),
            pl.BlockSpec((None, None, RET_DK, RET_DV), lambda b, h, r: (b, h, 0, 0)),
        ],
        out_shape=[
            jax.ShapeDtypeStruct((batch * seq, RET_HEADS * RET_DV), BF),
            jax.ShapeDtypeStruct((batch, RET_HEADS, RET_DK, RET_DV), F32),
        ],
        compiler_params=_cparams("parallel", "parallel", "arbitrary"),
        name="retention_prompt",
    )(proj, proj, proj, proj, cos, sin, lg)


def _ret_sample_kernel(p_ref, s0_ref, cos_ref, sin_ref, lg_ref, o_ref, s_ref, *, nb, seq):
    R = nb * seq
    ri = lax.broadcasted_iota(jnp.int32, (R, R), 0)
    rj = lax.broadcasted_iota(jnp.int32, (R, R), 1)
    same = (ri // seq) == (rj // seq)
    diff = (ri - rj).astype(F32)
    t_row = (lax.broadcasted_iota(jnp.int32, (R, 1), 0) % seq).astype(F32)
    b_row = lax.broadcasted_iota(jnp.int32, (R, 1), 0) // seq
    cos, sin = cos_ref[...], sin_ref[...]
    for h in range(RET_HEADS):
        lg = lg_ref[h]
        inner = jnp.where(same & (diff >= 0), jnp.exp(lg[:, :R] * jnp.maximum(diff, 0.0)), 0.0)
        cross = jnp.exp(lg * (t_row + 1.0))
        kdec = jnp.exp(lg[:, :RET_DK] * (seq - 1.0 - t_row))
        cdec = jnp.exp(lg * float(seq))
        q = _rotary(p_ref[:, h * RET_DK:(h + 1) * RET_DK], cos, sin)
        k = _rotary(p_ref[:, D_MODEL + h * RET_DK:D_MODEL + (h + 1) * RET_DK], cos, sin)
        k = k * (RET_DK ** -0.5)
        v = p_ref[:, 2 * D_MODEL + h * RET_DV:2 * D_MODEL + (h + 1) * RET_DV]
        g = p_ref[:, 4 * D_MODEL + h * RET_DV:4 * D_MODEL + (h + 1) * RET_DV]
        qb, vb = q.astype(BF), v.astype(BF)
        sc = lax.dot_general(qb, k.astype(BF), _NT, preferred_element_type=F32) * inner
        o = _dot(sc.astype(BF), vb)
        kd = k * kdec
        for b in range(nb):
            S = s0_ref[b, h]
            mine = b_row == b
            o = o + jnp.where(mine, _dot(qb, S.astype(BF)) * cross, 0.0)
            kb = jnp.where(mine, kd, 0.0).astype(BF)
            s_ref[b, h] = S * cdec + lax.dot_general(kb, vb, _TN, preferred_element_type=F32)
        o = o * lax.rsqrt(jnp.mean(o * o, axis=-1, keepdims=True) + EPS)
        o_ref[:, h * RET_DV:(h + 1) * RET_DV] = o * _silu(g)


def _ret_sample(proj, s0, layer, cos, sin, log_g, seq, nb):
    M = proj.shape[0]
    batch = M // seq
    R = nb * seq
    lg = jnp.broadcast_to(log_g[:, None, None], (RET_HEADS, 1, RET_DV))
    state = pl.BlockSpec((nb, RET_HEADS, RET_DK, RET_DV), lambda i: (i, 0, 0, 0))
    state_in = pl.BlockSpec((None, nb, RET_HEADS, RET_DK, RET_DV), lambda i: (layer, i, 0, 0, 0))
    return pl.pallas_call(
        functools.partial(_ret_sample_kernel, nb=nb, seq=seq),
        grid=(batch // nb,),
        in_specs=[
            pl.BlockSpec((R, 6 * D_MODEL), lambda i: (i, 0)),
            state_in,
            pl.BlockSpec((R, RET_DK // 2), lambda i: (0, 0)),
            pl.BlockSpec((R, RET_DK // 2), lambda i: (0, 0)),
            pl.BlockSpec((RET_HEADS, 1, RET_DV), lambda i: (0, 0, 0)),
        ],
        out_specs=[pl.BlockSpec((R, RET_HEADS * RET_DV), lambda i: (i, 0)), state],
        out_shape=[
            jax.ShapeDtypeStruct((M, RET_HEADS * RET_DV), F32),
            jax.ShapeDtypeStruct(s0.shape[1:], F32),
        ],
        compiler_params=_cparams("parallel"),
        name="retention_sample",
    )(proj, s0, jnp.tile(cos, (nb, 1)), jnp.tile(sin, (nb, 1)), lg)


def _fox_prompt_kernel(qi_ref, kj_ref, q_ref, k_ref, vo_ref, fq_ref, fk_ref, o_ref,
                       qm_ref, m_ref, acc_ref):
    tq = q_ref.shape[0]
    tk = k_ref.shape[0]
    nh = m_ref.shape[0]
    p = pl.program_id(2)
    hg = pl.program_id(1)
    qi = qi_ref[p]
    kj = kj_ref[p]

    def lanes(hh):
        return slice((hh // 2) * LANES, (hh // 2 + 1) * LANES)

    @pl.when(kj == 0)
    def _():
        lane = lax.broadcasted_iota(jnp.int32, (tq, LANES), 1)
        for hh in range(nh):
            q = q_ref[:, lanes(hh)].astype(F32)
            mine = (lane < FOX_HD) if hh % 2 == 0 else (lane >= FOX_HD)
            qm_ref[hh] = jnp.where(mine, q, 0.0).astype(BF)
        m_ref[...] = jnp.full_like(m_ref, NEG)
        acc_ref[...] = jnp.zeros_like(acc_ref)

    def step(diagonal):
        fk = fk_ref[...] * LOG2E
        head = lax.broadcasted_iota(jnp.int32, fk.shape, 1)
        for hh in range(nh):
            k = k_ref[:, lanes(hh)]
            vo = vo_ref[hh * LANES:(hh + 1) * LANES, :]
            fk_col = jnp.sum(jnp.where(head == nh * hg + hh, fk, 0.0), axis=1, keepdims=True)
            st = lax.dot_general(k, qm_ref[hh], _NT, preferred_element_type=F32) - fk_col
            if diagonal:
                krow = lax.broadcasted_iota(jnp.int32, (tk, tq), 0)
                qcol = lax.broadcasted_iota(jnp.int32, (tk, tq), 1)
                st = jnp.where(krow <= qcol, st, NEG)
            fq = fq_ref[hh:hh + 1, :] * LOG2E
            m_old = m_ref[hh]
            m_new = jnp.maximum(m_old, jnp.max(st, axis=0, keepdims=True) + fq)
            alpha = jnp.exp2(m_old - m_new)
            pm = jnp.exp2(st + (fq - m_new))
            acc_ref[hh] = alpha * acc_ref[hh] + _dot(vo, pm.astype(BF))
            m_ref[hh] = m_new

    @pl.when(kj < qi)
    def _():
        step(False)

    @pl.when(kj == qi)
    def _():
        step(True)
        for hh in range(0, nh, 2):
            even, odd = acc_ref[hh], acc_ref[hh + 1]
            ot = jnp.concatenate([even[:FOX_HD] / even[FOX_HD:FOX_HD + 1],
                                  odd[FOX_HD:] / odd[0:1]], axis=0)
            o_ref[:, lanes(hh)] = ot.T.astype(o_ref.dtype)


def _fox_prompt(q, k, vo, cum, cum_t, batch, seq, tq, nh):
    nq = seq // tq
    pairs = [(i, j) for i in range(nq) for j in range(i + 1)]
    qi_tab = jnp.asarray([p[0] for p in pairs], jnp.int32)
    kj_tab = jnp.asarray([p[1] for p in pairs], jnp.int32)
    groups = FOX_HEADS // nh
    w = nh * FOX_HD
    f_key = cum.reshape(batch, seq, FOX_HEADS)
    f_query = cum_t.reshape(batch, groups, nh, seq)
    grid_spec = pltpu.PrefetchScalarGridSpec(
        num_scalar_prefetch=2,
        grid=(batch, groups, len(pairs)),
        in_specs=[
            pl.BlockSpec((tq, w), lambda b, h, p, qi, kj: (b * nq + qi[p], h)),
            pl.BlockSpec((tq, w), lambda b, h, p, qi, kj: (b * nq + kj[p], h)),
            pl.BlockSpec((None, nh * LANES, tq), lambda b, h, p, qi, kj: (b, h, kj[p])),
            pl.BlockSpec((None, None, nh, tq), lambda b, h, p, qi, kj: (b, h, 0, qi[p])),
            pl.BlockSpec((None, tq, FOX_HEADS), lambda b, h, p, qi, kj: (b, kj[p], 0)),
        ],
        out_specs=pl.BlockSpec((tq, w), lambda b, h, p, qi, kj: (b * nq + qi[p], h)),
        scratch_shapes=[
            pltpu.VMEM((nh, tq, LANES), BF),
            pltpu.VMEM((nh, 1, tq), F32),
            pltpu.VMEM((nh, LANES, tq), F32),
        ],
    )
    return pl.pallas_call(
        _fox_prompt_kernel,
        grid_spec=grid_spec,
        out_shape=jax.ShapeDtypeStruct((batch * seq, D_MODEL), BF),
        compiler_params=_cparams("parallel", "parallel", "arbitrary"),
        name="fox_prompt_attention",
    )(qi_tab, kj_tab, q, k, vo, f_query, f_key)


def _fox_sample_kernel(pt_ref, q_ref, kn_ref, vn_ref, lfn_ref, *refs, npg, seq):
    kt_refs = refs[:npg]
    vt_refs = refs[npg:2 * npg]
    lf_refs = refs[2 * npg:3 * npg]
    o_ref = refs[3 * npg]
    qbd_ref, carry_ref, fn_ref, m_ref, l_ref, acc_ref, kpad_ref, vpad_ref = refs[3 * npg + 1:]
    g = pl.program_id(1)
    C = seq * FOX_HEADS
    P = kpad_ref.shape[0]
    row = lax.broadcasted_iota(jnp.int32, (C, D_MODEL), 0)
    col = lax.broadcasted_iota(jnp.int32, (C, D_MODEL), 1)
    own_head = (col >> 6) == (row % FOX_HEADS)
    t_of = lax.broadcasted_iota(jnp.int32, (C, P), 0) // FOX_HEADS
    j_of = lax.broadcasted_iota(jnp.int32, (C, P), 1)
    kk = lax.broadcasted_iota(jnp.int32, (P, P), 0)
    jj = lax.broadcasted_iota(jnp.int32, (P, P), 1)

    def tile_heads(a):
        return jnp.concatenate([a] * seq, axis=0)

    @pl.when(g == 0)
    def _():
        q = q_ref[...]
        qrep = jnp.concatenate(
            [jnp.broadcast_to(q[t:t + 1, :], (FOX_HEADS, D_MODEL)) for t in range(seq)], axis=0)
        qbd = jnp.where(own_head, qrep, 0.0).astype(BF)
        qbd_ref[...] = qbd
        kpad_ref[...] = jnp.zeros_like(kpad_ref)
        vpad_ref[...] = jnp.zeros_like(vpad_ref)
        kpad_ref[0:seq, :] = kn_ref[...]
        vpad_ref[0:seq, :] = vn_ref[...]
        fn = _dot_exact(lfn_ref[...], jnp.where(kk <= jj, 1.0, 0.0))
        fn_c = tile_heads(fn)
        fn_col = jnp.sum(jnp.where(j_of == t_of, fn_c, 0.0), axis=-1, keepdims=True)
        fn_ref[...] = fn_col
        s = lax.dot_general(qbd, kpad_ref[...].astype(BF), _NT, preferred_element_type=F32)
        s = jnp.where(j_of <= t_of, s + fn_col - fn_c, NEG)
        m = jnp.max(s, axis=-1, keepdims=True)
        pm = jnp.exp(s - m)
        m_ref[...] = m
        l_ref[...] = jnp.sum(pm, axis=-1, keepdims=True)
        acc_ref[...] = _dot(pm.astype(BF), vpad_ref[...].astype(BF))
        carry_ref[...] = jnp.zeros_like(carry_ref)

    later = jnp.where(kk > jj, 1.0, 0.0)
    qbd = qbd_ref[...]
    fn_col = fn_ref[...]
    carry = carry_ref[...]
    scores = []
    for i in reversed(range(npg)):
        lf = lf_refs[i][...]
        r_page = _dot_exact(lf, later) + carry
        carry = carry + jnp.sum(lf, axis=-1, keepdims=True)
        s = _dot(qbd, kt_refs[i][...].astype(BF))
        scores.append((i, s + fn_col + tile_heads(r_page)))
    carry_ref[...] = carry
    m_old = m_ref[...]
    m_new = m_old
    for _, s in scores:
        m_new = jnp.maximum(m_new, jnp.max(s, axis=-1, keepdims=True))
    alpha = jnp.exp(m_old - m_new)
    l_new = alpha * l_ref[...]
    acc = alpha * acc_ref[...]
    for i, s in scores:
        pm = jnp.exp(s - m_new)
        l_new = l_new + jnp.sum(pm, axis=-1, keepdims=True)
        acc = acc + lax.dot_general(pm.astype(BF), vt_refs[i][...].astype(BF), _NT,
                                    preferred_element_type=F32)
    m_ref[...] = m_new
    l_ref[...] = l_new
    acc_ref[...] = acc

    @pl.when(g == pl.num_programs(1) - 1)
    def _():
        on = jnp.where(own_head, acc / l_new, 0.0)
        for t in range(seq):
            o_ref[t:t + 1, :] = jnp.sum(on[t * FOX_HEADS:(t + 1) * FOX_HEADS, :], axis=0, keepdims=True)


def _fox_sample(q, k_new, v_new, lf_new_t, cache_kt, cache_vt, cache_lf_t, page_table, seq, npg):
    batch, n_pages = page_table.shape
    P = cache_kt.shape[-1]
    groups = n_pages // npg
    C = seq * FOX_HEADS

    def page(i):
        return lambda b, g, pt: (pt[b, (groups - 1 - g) * npg + i], 0, 0)

    new_tok = pl.BlockSpec((None, seq, D_MODEL), lambda b, g, pt: (b, 0, 0))
    grid_spec = pltpu.PrefetchScalarGridSpec(
        num_scalar_prefetch=1,
        grid=(batch, groups),
        in_specs=(
            [new_tok, new_tok, new_tok,
             pl.BlockSpec((None, FOX_HEADS, P), lambda b, g, pt: (b, 0, 0))]
            + [pl.BlockSpec((None, D_MODEL, P), page(i)) for i in range(npg)]
            + [pl.BlockSpec((None, D_MODEL, P), page(i)) for i in range(npg)]
            + [pl.BlockSpec((None, FOX_HEADS, P), page(i)) for i in range(npg)]
        ),
        out_specs=new_tok,
        scratch_shapes=[
            pltpu.VMEM((C, D_MODEL), BF),
            pltpu.VMEM((FOX_HEADS, 1), F32),
            pltpu.VMEM((C, 1), F32),
            pltpu.VMEM((C, 1), F32),
            pltpu.VMEM((C, 1), F32),
            pltpu.VMEM((C, D_MODEL), F32),
            pltpu.VMEM((P, D_MODEL), F32),
            pltpu.VMEM((P, D_MODEL), F32),
        ],
    )
    return pl.pallas_call(
        functools.partial(_fox_sample_kernel, npg=npg, seq=seq),
        grid_spec=grid_spec,
        out_shape=jax.ShapeDtypeStruct((batch, seq, D_MODEL), F32),
        compiler_params=_cparams("parallel", "arbitrary"),
        name="fox_sample_attention",
    )(page_table, q, k_new, v_new, lf_new_t,
      *([cache_kt] * npg), *([cache_vt] * npg), *([cache_lf_t] * npg))


def _prep_weights(w_ret_in, w_ret_out, w_q, w_o, w_ffn_in, w_ffn_out, w_kvf):
    depth, _, two_f = w_ffn_in.shape
    d_ff = two_f // 2
    nf = d_ff // FFN_CHUNK
    wi = w_ffn_in.astype(BF).reshape(depth, D_MODEL, 2 * nf, FFN_CHUNK).transpose(0, 2, 1, 3)
    wo = w_ffn_out.astype(BF).reshape(depth, nf, FFN_CHUNK, D_MODEL)
    w_kvf_b = w_kvf.astype(BF)
    wf = jnp.zeros((D_MODEL, LANES), BF).at[:, :FOX_HEADS].set(w_kvf_b[:, 2 * D_MODEL:])
    return dict(
        ret_in=w_ret_in.astype(BF), ret_out=w_ret_out.astype(BF), q=w_q.astype(BF), o=w_o.astype(BF),
        ffn_in=wi, ffn_out=wo, wk=w_kvf_b[:, :D_MODEL], wv=w_kvf_b[:, D_MODEL:2 * D_MODEL], wf=wf)


def _rope_tables(pos):
    half = RET_DK // 2
    freqs = ROPE_BASE ** (-jnp.arange(half, dtype=F32) / half)
    ang = pos.astype(F32)[:, None] * freqs[None, :]
    return jnp.cos(ang), jnp.sin(ang)


def _trunk(x, mods, kvmod, mod, pos, W, norms, b_f, *, tm, kvf_seq, ret_fn, attn_fn):
    ret_norm_g, attn_norm_g, q_norm_g, ffn_norm_g, kv_norm_g, k_norm_g = norms
    n_a = W["ret_in"].shape[0]
    depth = W["ffn_in"].shape[0]
    log_g = jnp.log1p(-jnp.exp2(-5.0 - jnp.arange(RET_HEADS, dtype=F32)))
    cos, sin = _rope_tables(pos)
    states = []
    shared = None
    for l in range(depth):
        sh1, sc1, g1, sh2, sc2, g2 = mods[l]
        if l < n_a:
            a, S = ret_fn(l, x, ret_norm_g[l], sh1, sc1, cos, sin, log_g)
            states.append(S)
            x = _out_proj(a, W["ret_out"], l, x, g1, mod, tm)
        else:
            j = l - n_a
            if shared is None:
                shared = _kvf_proj(x, kv_norm_g, kvmod[0], kvmod[1], W["wk"], W["wv"], W["wf"],
                                   b_f, k_norm_g, mod, min(tm, 512), kvf_seq)
            a = attn_fn(j, x, attn_norm_g[j], sh1, sc1, q_norm_g[j], shared)
            x = _out_proj(a, W["o"], j, x, g1, mod, tm)
        x = _ffn(x, ffn_norm_g[l], sh2, sc2, g2, W["ffn_in"], W["ffn_out"], l, mod, tm)
    return x, jnp.stack(states), shared


def kernel(x_prompt, x_sample, cache_k, cache_v, cache_logf, state_ret, page_table, c_prompt, c_sample,
           w_mod, b_mod, ret_norm_g, w_ret_in, w_ret_out, attn_norm_g, w_q, q_norm_g, w_o, ffn_norm_g,
           w_ffn_in, w_ffn_out, kv_norm_g, w_kvmod, b_kvmod, w_kvf, b_f, k_norm_g):
    Bp, Tp, _ = x_prompt.shape
    Bd, Td, _ = x_sample.shape
    depth = w_mod.shape[0]
    n_pool, page_size = cache_k.shape[:2]
    past_len = page_table.shape[1] * page_size
    W = _prep_weights(w_ret_in, w_ret_out, w_q, w_o, w_ffn_in, w_ffn_out, w_kvf)
    norms = (ret_norm_g, attn_norm_g, q_norm_g, ffn_norm_g, kv_norm_g, k_norm_g)

    c_all = jnp.concatenate([c_prompt, c_sample], axis=0)
    mod_all = _cond_proj(c_all, w_mod, b_mod, 1536)
    kvmod_all = _cond_proj(c_all, w_kvmod[None], b_kvmod[None], 1024)[0]

    def split_mods(lo, hi, expand):
        def lay(a):
            return jnp.repeat(a, expand, axis=0)[None] if expand else a[:, None, :]
        mods = [[lay(m) for m in jnp.split(mod_all[l, lo:hi], 6, axis=-1)] for l in range(depth)]
        kvmod = [lay(m) for m in jnp.split(kvmod_all[lo:hi], 2, axis=-1)]
        return mods, kvmod

    mods_p, kvmod_p = split_mods(0, Bp, 0)
    mod_p = _Mod(Tp, per_row=False)
    tm_p = 1024

    def ret_prompt(l, x, g, sh, sc, cos, sin, log_g):
        proj = _nm_matmul(x, g, sh, sc, W["ret_in"], l, mod_p, tm_p, 3072, BF)
        return _ret_prompt(proj, cos, sin, log_g, Bp, Tp, 2048)

    def attn_prompt(j, x, g, sh, sc, gq, shared):
        _, _, _, cum_t, cum, kb, vo = shared
        q = _q_proj(x, g, sh, sc, W["q"], j, gq, mod_p, tm_p, BF, FOX_HD ** -0.5 * LOG2E)
        return _fox_prompt(q, kb, vo, cum, cum_t, Bp, Tp, 1024, 8)

    y_p, st_p, (kt_p, vt_p, lft_p, _, _, _, _) = _trunk(
        x_prompt.reshape(Bp * Tp, D_MODEL), mods_p, kvmod_p, mod_p, jnp.arange(Tp), W, norms, b_f,
        tm=tm_p, kvf_seq=Tp, ret_fn=ret_prompt, attn_fn=attn_prompt)

    mods_s, kvmod_s = split_mods(Bp, Bp + Bd, Td)
    Ms = Bd * Td
    mod_s = _Mod(Ms, per_row=True)
    ckt = cache_k.transpose(0, 2, 3, 1).reshape(n_pool, D_MODEL, page_size)
    cvt = cache_v.transpose(0, 2, 3, 1).reshape(n_pool, D_MODEL, page_size)
    clf_t = cache_logf.transpose(0, 2, 1)

    def ret_sample(l, x, g, sh, sc, cos, sin, log_g):
        proj = _nm_matmul(x, g, sh, sc, W["ret_in"], l, mod_s, Ms, 1536, F32)
        return _ret_sample(proj, state_ret, l, cos, sin, log_g, Td, 2)

    def attn_sample(j, x, g, sh, sc, gq, shared):
        k, v, lf = shared
        q = _q_proj(x, g, sh, sc, W["q"], j, gq, mod_s, Ms, F32, FOX_HD ** -0.5)
        lf_t = jnp.pad(lf.reshape(Bd, Td, FOX_HEADS).transpose(0, 2, 1),
                       ((0, 0), (0, 0), (0, page_size - Td)))
        o = _fox_sample(q.reshape(Bd, Td, D_MODEL), k.reshape(Bd, Td, D_MODEL), v.reshape(Bd, Td, D_MODEL),
                        lf_t, ckt, cvt, clf_t, page_table, Td, 8 if j == 0 else 16)
        return o.reshape(Ms, D_MODEL)

    y_s, st_s, (k_s, v_s, lf_s) = _trunk(
        x_sample.reshape(Ms, D_MODEL), mods_s, kvmod_s, mod_s, past_len + jnp.arange(Td), W, norms, b_f,
        tm=Ms, kvf_seq=None, ret_fn=ret_sample, attn_fn=attn_sample)

    heads = (FOX_HEADS, FOX_HD)
    k_p = kt_p.reshape(Bp, *heads, Tp).transpose(0, 3, 1, 2)
    v_p = vt_p.reshape(Bp, *heads, Tp).transpose(0, 3, 1, 2)
    return (y_p.reshape(Bp, Tp, D_MODEL), y_s.reshape(Bd, Td, D_MODEL), st_p, st_s,
            k_p, v_p, lft_p.transpose(0, 2, 1),
            k_s.reshape(Bd, Td, *heads), v_s.reshape(Bd, Td, *heads), lf_s.reshape(Bd, Td, FOX_HEADS))
```

```python
import functools
import math

import jax
import jax.numpy as jnp
from jax import lax
from jax.experimental import pallas as pl
from jax.experimental.pallas import tpu as pltpu

BF = jnp.bfloat16
F32 = jnp.float32

D_MODEL = 1024
RET_HEADS = 4
RET_DK = 256
RET_DV = 512
RET_CHUNK = 128
ROPE_BASE = 10000.0
FOX_HEADS = 16
FOX_HD = 64
EPS = 1e-6
LANES = 128
VMEM_LIMIT = 52 * 1024 * 1024
NEG = -1e30
LOG2E = math.log2(math.e)

_NT = (((1,), (1,)), ((), ()))
_TN = (((0,), (0,)), ((), ()))


def _cparams(*sem):
    return pltpu.CompilerParams(dimension_semantics=sem, vmem_limit_bytes=VMEM_LIMIT)


def _dot(a, b):
    return jnp.dot(a, b, preferred_element_type=F32)


def _dot_exact(a, b):
    return jnp.dot(a, b, preferred_element_type=F32, precision=lax.Precision.HIGHEST)


def _silu(x):
    return x * jax.nn.sigmoid(x)


def _normmod(x, g, shift, scale):
    ms = jnp.mean(x * x, axis=-1, keepdims=True)
    y = x * lax.rsqrt(ms + EPS) * g
    return y * (1.0 + scale) + shift


def _split_bf16(x):
    hi = x.astype(BF)
    lo = (x - hi.astype(F32)).astype(BF)
    return hi, lo


def _headnorm(a):
    r = lax.broadcasted_iota(jnp.int32, (D_MODEL, LANES), 0) >> 6
    c = lax.broadcasted_iota(jnp.int32, (D_MODEL, LANES), 1)
    seg = jnp.where(r == c, 1.0, 0.0).astype(BF)
    rt = lax.broadcasted_iota(jnp.int32, (LANES, D_MODEL), 0)
    ct = lax.broadcasted_iota(jnp.int32, (LANES, D_MODEL), 1) >> 6
    seg_t = jnp.where(rt == ct, 1.0, 0.0).astype(BF)
    y_hi, y_lo = _split_bf16(a * a)
    ms = (_dot(y_hi, seg) + _dot(y_lo, seg)) * (1.0 / FOX_HD)
    rs = lax.rsqrt(ms + EPS)
    r_hi, r_lo = _split_bf16(rs)
    return a * (_dot(r_hi, seg_t) + _dot(r_lo, seg_t))


def _log_sigmoid(x):
    return jnp.minimum(x, 0.0) - jnp.log1p(jnp.exp(-jnp.abs(x)))


def _cond_kernel(c_ref, w_ref, b_ref, o_ref):
    ca = _silu(c_ref[...]).astype(BF)
    o_ref[...] = _dot(ca, w_ref[...].astype(BF)) + b_ref[...]


def _cond_proj(c, w, b, tn):
    L, K, N = w.shape
    R = c.shape[0]
    return pl.pallas_call(
        _cond_kernel,
        grid=(L, N // tn),
        in_specs=[
            pl.BlockSpec((R, K), lambda l, j: (0, 0)),
            pl.BlockSpec((None, K, tn), lambda l, j: (l, 0, j)),
            pl.BlockSpec((None, 1, tn), lambda l, j: (l, 0, j)),
        ],
        out_specs=pl.BlockSpec((None, R, tn), lambda l, j: (l, 0, j)),
        out_shape=jax.ShapeDtypeStruct((L, R, N), F32),
        compiler_params=_cparams("parallel", "parallel"),
        name="cond_proj",
    )(c, w, b.reshape(L, 1, N))


class _Mod:
    def __init__(self, rows_per_group, per_row):
        self.rows_per_group = rows_per_group
        self.per_row = per_row

    def spec(self, tm):
        if self.per_row:
            return pl.BlockSpec((None, tm, D_MODEL), lambda i, *_: (0, i, 0))
        tiles = self.rows_per_group // tm
        return pl.BlockSpec((None, 1, D_MODEL), lambda i, *_: (i // tiles, 0, 0))


def _vec_spec(n):
    return pl.BlockSpec((1, n), lambda i, *_: (0, 0))


def _nm_kernel(x_ref, g_ref, sh_ref, sc_ref, w_ref, o_ref, h_ref):
    @pl.when(pl.program_id(1) == 0)
    def _():
        h_ref[...] = _normmod(x_ref[...], g_ref[...], sh_ref[...], sc_ref[...]).astype(BF)

    o_ref[...] = _dot(h_ref[...], w_ref[...]).astype(o_ref.dtype)


def _nm_matmul(x, g, shift, scale, w, layer, mod, tm, tn, out_dtype):
    M, K = x.shape
    N = w.shape[2]
    return pl.pallas_call(
        _nm_kernel,
        grid=(M // tm, N // tn),
        in_specs=[
            pl.BlockSpec((tm, K), lambda i, j: (i, 0)),
            _vec_spec(K),
            mod.spec(tm),
            mod.spec(tm),
            pl.BlockSpec((None, K, tn), lambda i, j: (layer, 0, j)),
        ],
        out_specs=pl.BlockSpec((tm, tn), lambda i, j: (i, j)),
        out_shape=jax.ShapeDtypeStruct((M, N), out_dtype),
        scratch_shapes=[pltpu.VMEM((tm, K), BF)],
        compiler_params=_cparams("parallel", "arbitrary"),
        name="norm_mod_matmul",
    )(x, g.reshape(1, K), shift, scale, w)


def _q_kernel(x_ref, g_ref, sh_ref, sc_ref, w_ref, gq_ref, o_ref, *, q_scale):
    h = _normmod(x_ref[...], g_ref[...], sh_ref[...], sc_ref[...]).astype(BF)
    q = _headnorm(_dot(h, w_ref[...])) * gq_ref[...]
    o_ref[...] = (q * q_scale).astype(o_ref.dtype)


def _q_proj(x, g, shift, scale, w, layer, gq, mod, tm, out_dtype, q_scale):
    M, K = x.shape
    return pl.pallas_call(
        functools.partial(_q_kernel, q_scale=q_scale),
        grid=(M // tm,),
        in_specs=[
            pl.BlockSpec((tm, K), lambda i: (i, 0)),
            _vec_spec(K),
            mod.spec(tm),
            mod.spec(tm),
            pl.BlockSpec((None, K, D_MODEL), lambda i: (layer, 0, 0)),
            _vec_spec(D_MODEL),
        ],
        out_specs=pl.BlockSpec((tm, D_MODEL), lambda i: (i, 0)),
        out_shape=jax.ShapeDtypeStruct((M, D_MODEL), out_dtype),
        compiler_params=_cparams("parallel"),
        name="q_proj_headnorm",
    )(x, g.reshape(1, K), shift, scale, w, jnp.tile(gq, FOX_HEADS).reshape(1, D_MODEL))


def _kvf_core(x_ref, g_ref, sh_ref, sc_ref, wk_ref, wv_ref, wf_ref, bf_ref, gk_ref):
    h = _normmod(x_ref[...], g_ref[...], sh_ref[...], sc_ref[...]).astype(BF)
    k = _headnorm(_dot(h, wk_ref[...])) * gk_ref[...]
    v = _dot(h, wv_ref[...])
    lf = _log_sigmoid(_dot(h, wf_ref[...]) + bf_ref[...])
    return k, v, lf


def _kvf_rows_kernel(*refs):
    k_ref, v_ref, lf_ref = refs[9:]
    k, v, lf = _kvf_core(*refs[:9])
    k_ref[...] = k
    v_ref[...] = v
    lf_ref[...] = lf[:, :FOX_HEADS]


def _kvf_seq_kernel(*refs, tiles_per_seq):
    kt_ref, vt_ref, lft_ref, cumt_ref, cum_ref, kb_ref, vto_ref, carry_ref = refs[9:]
    tm = kb_ref.shape[0]
    k, v, lf = _kvf_core(*refs[:9])
    kt_ref[...] = k.T
    vt = v.T
    vt_ref[...] = vt
    vto_ref[...] = jnp.ones_like(vto_ref)
    for h in range(FOX_HEADS):
        dst = h * LANES + (h % 2) * FOX_HD
        vto_ref[dst:dst + FOX_HD, :] = vt[h * FOX_HD:(h + 1) * FOX_HD, :].astype(BF)
    kb_ref[...] = k.astype(BF)
    lft_ref[...] = lf.T[:FOX_HEADS, :]

    @pl.when(pl.program_id(0) % tiles_per_seq == 0)
    def _():
        carry_ref[...] = jnp.zeros_like(carry_ref)

    r = lax.broadcasted_iota(jnp.int32, (tm, tm), 0)
    c = lax.broadcasted_iota(jnp.int32, (tm, tm), 1)
    tri = jnp.where(r >= c, 1.0, 0.0)
    cum = _dot_exact(tri, lf) + carry_ref[...]
    carry_ref[...] = cum[tm - 1:tm, :]
    cum_ref[...] = cum[:, :FOX_HEADS]
    cumt_ref[...] = cum.T[:FOX_HEADS, :]


def _kvf_proj(x, g, shift, scale, wk, wv, wf, b_f, gk, mod, tm, seq_len=None):
    M, K = x.shape
    tok = pl.BlockSpec((tm, D_MODEL), lambda i: (i, 0))
    hd = pl.BlockSpec((tm, FOX_HEADS), lambda i: (i, 0))
    w_spec = pl.BlockSpec((K, D_MODEL), lambda i: (0, 0))
    bf_pad = jnp.zeros((1, LANES), F32).at[0, :FOX_HEADS].set(b_f)
    in_specs = [
        pl.BlockSpec((tm, K), lambda i: (i, 0)),
        _vec_spec(K),
        mod.spec(tm),
        mod.spec(tm),
        w_spec,
        w_spec,
        pl.BlockSpec((K, LANES), lambda i: (0, 0)),
        _vec_spec(LANES),
        _vec_spec(D_MODEL),
    ]
    args = (x, g.reshape(1, K), shift, scale, wk, wv, wf, bf_pad, jnp.tile(gk, FOX_HEADS).reshape(1, D_MODEL))
    if seq_len is None:
        return pl.pallas_call(
            _kvf_rows_kernel,
            grid=(M // tm,),
            in_specs=in_specs,
            out_specs=[tok, tok, hd],
            out_shape=[
                jax.ShapeDtypeStruct((M, D_MODEL), F32),
                jax.ShapeDtypeStruct((M, D_MODEL), F32),
                jax.ShapeDtypeStruct((M, FOX_HEADS), F32),
            ],
            compiler_params=_cparams("parallel"),
            name="shared_kvf_rows",
        )(*args)
    tiles = seq_len // tm
    batch = M // seq_len
    feat_t = pl.BlockSpec((None, D_MODEL, tm), lambda i: (i // tiles, 0, i % tiles))
    head_t = pl.BlockSpec((None, FOX_HEADS, tm), lambda i: (i // tiles, 0, i % tiles))
    return pl.pallas_call(
        functools.partial(_kvf_seq_kernel, tiles_per_seq=tiles),
        grid=(M // tm,),
        in_specs=in_specs,
        out_specs=[feat_t, feat_t, head_t, head_t, hd, tok,
                   pl.BlockSpec((None, FOX_HEADS * LANES, tm), lambda i: (i // tiles, 0, i % tiles))],
        out_shape=[
            jax.ShapeDtypeStruct((batch, D_MODEL, seq_len), F32),
            jax.ShapeDtypeStruct((batch, D_MODEL, seq_len), F32),
            jax.ShapeDtypeStruct((batch, FOX_HEADS, seq_len), F32),
            jax.ShapeDtypeStruct((batch, FOX_HEADS, seq_len), F32),
            jax.ShapeDtypeStruct((M, FOX_HEADS), F32),
            jax.ShapeDtypeStruct((M, D_MODEL), BF),
            jax.ShapeDtypeStruct((batch, FOX_HEADS * LANES, seq_len), BF),
        ],
        scratch_shapes=[pltpu.VMEM((1, LANES), F32)],
        compiler_params=_cparams("arbitrary"),
        name="shared_kvf_seq",
    )(*args)


def _out_kernel(a_ref, w_ref, x_ref, gt_ref, o_ref):
    y = _dot(a_ref[...].astype(BF), w_ref[...])
    o_ref[...] = x_ref[...] + (1.0 + gt_ref[...]) * y


def _out_proj(a, w, layer, x, gate, mod, tm):
    M, K = a.shape
    return pl.pallas_call(
        _out_kernel,
        grid=(M // tm,),
        in_specs=[
            pl.BlockSpec((tm, K), lambda i: (i, 0)),
            pl.BlockSpec((None, K, D_MODEL), lambda i: (layer, 0, 0)),
            pl.BlockSpec((tm, D_MODEL), lambda i: (i, 0)),
            mod.spec(tm),
        ],
        out_specs=pl.BlockSpec((tm, D_MODEL), lambda i: (i, 0)),
        out_shape=jax.ShapeDtypeStruct((M, D_MODEL), F32),
        compiler_params=_cparams("parallel"),
        name="out_proj_residual",
    )(a, w, x, gate)


FFN_CHUNK = 256


def _ffn_kernel(x_ref, g_ref, sh_ref, sc_ref, gt_ref, wi_ref, wo_ref, o_ref, h_ref, acc_ref, *, nf):
    h_ref[...] = _normmod(x_ref[...], g_ref[...], sh_ref[...], sc_ref[...]).astype(BF)
    acc_ref[...] = jnp.zeros_like(acc_ref)

    def body(f, carry):
        h = h_ref[...]
        gate = _dot(h, wi_ref[f])
        up = _dot(h, wi_ref[nf + f])
        act = (_silu(gate) * up).astype(BF)
        acc_ref[...] += _dot(act, wo_ref[f])
        return carry

    lax.fori_loop(0, nf, body, 0)
    o_ref[...] = x_ref[...] + (1.0 + gt_ref[...]) * acc_ref[...]


def _ffn(x, g, shift, scale, gate, wi, wo, layer, mod, tm):
    M, K = x.shape
    nf = wo.shape[1]
    resident = pl.Buffered(1)
    return pl.pallas_call(
        functools.partial(_ffn_kernel, nf=nf),
        grid=(M // tm,),
        in_specs=[
            pl.BlockSpec((tm, K), lambda i: (i, 0)),
            _vec_spec(K),
            mod.spec(tm),
            mod.spec(tm),
            mod.spec(tm),
            pl.BlockSpec((None,) + wi.shape[1:], lambda i: (layer, 0, 0, 0), pipeline_mode=resident),
            pl.BlockSpec((None,) + wo.shape[1:], lambda i: (layer, 0, 0, 0), pipeline_mode=resident),
        ],
        out_specs=pl.BlockSpec((tm, K), lambda i: (i, 0)),
        out_shape=jax.ShapeDtypeStruct((M, K), F32),
        scratch_shapes=[pltpu.VMEM((tm, K), BF), pltpu.VMEM((tm, K), F32)],
        compiler_params=_cparams("parallel"),
        name="ffn_swiglu",
    )(x, g.reshape(1, K), shift, scale, gate, wi, wo)


def _rotary(x, cos, sin):
    half = x.shape[-1] // 2
    x1, x2 = x[:, :half], x[:, half:]
    return jnp.concatenate([x1 * cos - x2 * sin, x1 * sin + x2 * cos], axis=-1)


def _ret_prompt_kernel(q_ref, k_ref, v_ref, g_ref, cos_ref, sin_ref, lg_ref, o_ref, s_ref, *, rows):
    C = RET_CHUNK

    @pl.when(pl.program_id(2) == 0)
    def _():
        s_ref[...] = jnp.zeros_like(s_ref)

    lg = lg_ref[...]
    i_v = lax.broadcasted_iota(jnp.int32, (C, RET_DV), 0).astype(F32)
    i_k = lax.broadcasted_iota(jnp.int32, (C, RET_DK), 0).astype(F32)
    cross = jnp.exp(lg * (i_v + 1.0))
    kdec = jnp.exp(lg_ref[:, :RET_DK] * (C - 1.0 - i_k))
    cdec = jnp.exp(lg * float(C))
    ii = lax.broadcasted_iota(jnp.int32, (C, C), 0)
    jj = lax.broadcasted_iota(jnp.int32, (C, C), 1)
    diff = (ii - jj).astype(F32)
    inner = jnp.where(diff >= 0, jnp.exp(lg_ref[:, :C] * jnp.maximum(diff, 0.0)), 0.0)

    for c in range(rows // C):
        sl = slice(c * C, (c + 1) * C)
        cos, sin = cos_ref[sl, :], sin_ref[sl, :]
        q = _rotary(q_ref[sl, :].astype(F32), cos, sin)
        k = _rotary(k_ref[sl, :].astype(F32), cos, sin) * (RET_DK ** -0.5)
        v = v_ref[sl, :]
        qb = q.astype(BF)
        S = s_ref[...]
        sc = lax.dot_general(qb, k.astype(BF), _NT, preferred_element_type=F32) * inner
        o = _dot(sc.astype(BF), v) + _dot(qb, S.astype(BF)) * cross
        s_ref[...] = S * cdec + lax.dot_general((k * kdec).astype(BF), v, _TN,
                                                preferred_element_type=F32)
        o = o * lax.rsqrt(jnp.mean(o * o, axis=-1, keepdims=True) + EPS)
        o_ref[sl, :] = (o * _silu(g_ref[sl, :].astype(F32))).astype(o_ref.dtype)


def _ret_prompt(proj, cos, sin, log_g, batch, seq, rows):
    nr = seq // rows
    kq = D_MODEL // RET_DK
    kv = 2 * D_MODEL // RET_DV
    kg = 4 * D_MODEL // RET_DV
    row = lambda b, h, r: b * nr + r
    lg = jnp.broadcast_to(log_g[:, None, None], (RET_HEADS, 1, RET_DV))
    return pl.pallas_call(
        functools.partial(_ret_prompt_kernel, rows=rows),
        grid=(batch, RET_HEADS, nr),
        in_specs=[
            pl.BlockSpec((rows, RET_DK), lambda b, h, r: (row(b, h, r), h)),
            pl.BlockSpec((rows, RET_DK), lambda b, h, r: (row(b, h, r), kq + h)),
            pl.BlockSpec((rows, RET_DV), lambda b, h, r: (row(b, h, r), kv + h)),
            pl.BlockSpec((rows, RET_DV), lambda b, h, r: (row(b, h, r), kg + h)),
            pl.BlockSpec((rows, RET_DK // 2), lambda b, h, r: (r, 0)),
            pl.BlockSpec((rows, RET_DK // 2), lambda b, h, r: (r, 0)),
            pl.BlockSpec((None, 1, RET_DV), lambda b, h, r: (h, 0, 0)),
        ],
        out_specs=[
            pl.BlockSpec((rows, RET_DV), lambda b, h, r: (row(b, h, r), h)),
            pl.BlockSpec((None, None, RET_DK, RET_DV), lambda b, h, r: (b, h, 0, 0)),
        ],
        out_shape=[
            jax.ShapeDtypeStruct((batch * seq, RET_HEADS * RET_DV), BF),
            jax.ShapeDtypeStruct((batch, RET_HEADS, RET_DK, RET_DV), F32),
        ],
        compiler_params=_cparams("parallel", "parallel", "arbitrary"),
        name="retention_prompt",
    )(proj, proj, proj, proj, cos, sin, lg)


def _ret_sample_kernel(p_ref, s0_ref, cos_ref, sin_ref, lg_ref, o_ref, s_ref, *, nb, seq):
    R = nb * seq
    ri = lax.broadcasted_iota(jnp.int32, (R, R), 0)
    rj = lax.broadcasted_iota(jnp.int32, (R, R), 1)
    same = (ri // seq) == (rj // seq)
    diff = (ri - rj).astype(F32)
    t_row = (lax.broadcasted_iota(jnp.int32, (R, 1), 0) % seq).astype(F32)
    b_row = lax.broadcasted_iota(jnp.int32, (R, 1), 0) // seq
    cos, sin = cos_ref[...], sin_ref[...]
    for h in range(RET_HEADS):
        lg = lg_ref[h]
        inner = jnp.where(same & (diff >= 0), jnp.exp(lg[:, :R] * jnp.maximum(diff, 0.0)), 0.0)
        cross = jnp.exp(lg * (t_row + 1.0))
        kdec = jnp.exp(lg[:, :RET_DK] * (seq - 1.0 - t_row))
        cdec = jnp.exp(lg * float(seq))
        q = _rotary(p_ref[:, h * RET_DK:(h + 1) * RET_DK], cos, sin)
        k = _rotary(p_ref[:, D_MODEL + h * RET_DK:D_MODEL + (h + 1) * RET_DK], cos, sin)
        k = k * (RET_DK ** -0.5)
        v = p_ref[:, 2 * D_MODEL + h * RET_DV:2 * D_MODEL + (h + 1) * RET_DV]
        g = p_ref[:, 4 * D_MODEL + h * RET_DV:4 * D_MODEL + (h + 1) * RET_DV]
        qb, vb = q.astype(BF), v.astype(BF)
        sc = lax.dot_general(qb, k.astype(BF), _NT, preferred_element_type=F32) * inner
        o = _dot(sc.astype(BF), vb)
        kd = k * kdec
        for b in range(nb):
            S = s0_ref[b, h]
            mine = b_row == b
            o = o + jnp.where(mine, _dot(qb, S.astype(BF)) * cross, 0.0)
            kb = jnp.where(mine, kd, 0.0).astype(BF)
            s_ref[b, h] = S * cdec + lax.dot_general(kb, vb, _TN, preferred_element_type=F32)
        o = o * lax.rsqrt(jnp.mean(o * o, axis=-1, keepdims=True) + EPS)
        o_ref[:, h * RET_DV:(h + 1) * RET_DV] = o * _silu(g)


def _ret_sample(proj, s0, layer, cos, sin, log_g, seq, nb):
    M = proj.shape[0]
    batch = M // seq
    R = nb * seq
    lg = jnp.broadcast_to(log_g[:, None, None], (RET_HEADS, 1, RET_DV))
    state = pl.BlockSpec((nb, RET_HEADS, RET_DK, RET_DV), lambda i: (i, 0, 0, 0))
    state_in = pl.BlockSpec((None, nb, RET_HEADS, RET_DK, RET_DV), lambda i: (layer, i, 0, 0, 0))
    return pl.pallas_call(
        functools.partial(_ret_sample_kernel, nb=nb, seq=seq),
        grid=(batch // nb,),
        in_specs=[
            pl.BlockSpec((R, 6 * D_MODEL), lambda i: (i, 0)),
            state_in,
            pl.BlockSpec((R, RET_DK // 2), lambda i: (0, 0)),
            pl.BlockSpec((R, RET_DK // 2), lambda i: (0, 0)),
            pl.BlockSpec((RET_HEADS, 1, RET_DV), lambda i: (0, 0, 0)),
        ],
        out_specs=[pl.BlockSpec((R, RET_HEADS * RET_DV), lambda i: (i, 0)), state],
        out_shape=[
            jax.ShapeDtypeStruct((M, RET_HEADS * RET_DV), F32),
            jax.ShapeDtypeStruct(s0.shape[1:], F32),
        ],
        compiler_params=_cparams("parallel"),
        name="retention_sample",
    )(proj, s0, jnp.tile(cos, (nb, 1)), jnp.tile(sin, (nb, 1)), lg)


def _fox_prompt_kernel(qi_ref, kj_ref, q_ref, k_ref, vo_ref, fq_ref, fk_ref, o_ref,
                       qm_ref, m_ref, acc_ref):
    tq = q_ref.shape[0]
    tk = k_ref.shape[0]
    nh = m_ref.shape[0]
    p = pl.program_id(2)
    hg = pl.program_id(1)
    qi = qi_ref[p]
    kj = kj_ref[p]

    def lanes(hh):
        return slice((hh // 2) * LANES, (hh // 2 + 1) * LANES)

    @pl.when(kj == 0)
    def _():
        lane = lax.broadcasted_iota(jnp.int32, (tq, LANES), 1)
        for hh in range(nh):
            q = q_ref[:, lanes(hh)].astype(F32)
            mine = (lane < FOX_HD) if hh % 2 == 0 else (lane >= FOX_HD)
            qm_ref[hh] = jnp.where(mine, q, 0.0).astype(BF)
        m_ref[...] = jnp.full_like(m_ref, NEG)
        acc_ref[...] = jnp.zeros_like(acc_ref)

    def step(diagonal):
        fk = fk_ref[...] * LOG2E
        head = lax.broadcasted_iota(jnp.int32, fk.shape, 1)
        for hh in range(nh):
            k = k_ref[:, lanes(hh)]
            vo = vo_ref[hh * LANES:(hh + 1) * LANES, :]
            fk_col = jnp.sum(jnp.where(head == nh * hg + hh, fk, 0.0), axis=1, keepdims=True)
            st = lax.dot_general(k, qm_ref[hh], _NT, preferred_element_type=F32) - fk_col
            if diagonal:
                krow = lax.broadcasted_iota(jnp.int32, (tk, tq), 0)
                qcol = lax.broadcasted_iota(jnp.int32, (tk, tq), 1)
                st = jnp.where(krow <= qcol, st, NEG)
            fq = fq_ref[hh:hh + 1, :] * LOG2E
            m_old = m_ref[hh]
            m_new = jnp.maximum(m_old, jnp.max(st, axis=0, keepdims=True) + fq)
            alpha = jnp.exp2(m_old - m_new)
            pm = jnp.exp2(st + (fq - m_new))
            acc_ref[hh] = alpha * acc_ref[hh] + _dot(vo, pm.astype(BF))
            m_ref[hh] = m_new

    @pl.when(kj < qi)
    def _():
        step(False)

    @pl.when(kj == qi)
    def _():
        step(True)
        for hh in range(0, nh, 2):
            even, odd = acc_ref[hh], acc_ref[hh + 1]
            ot = jnp.concatenate([even[:FOX_HD] / even[FOX_HD:FOX_HD + 1],
                                  odd[FOX_HD:] / odd[0:1]], axis=0)
            o_ref[:, lanes(hh)] = ot.T.astype(o_ref.dtype)


def _fox_prompt(q, k, vo, cum, cum_t, batch, seq, tq, nh):
    nq = seq // tq
    pairs = [(i, j) for i in range(nq) for j in range(i + 1)]
    qi_tab = jnp.asarray([p[0] for p in pairs], jnp.int32)
    kj_tab = jnp.asarray([p[1] for p in pairs], jnp.int32)
    groups = FOX_HEADS // nh
    w = nh * FOX_HD
    f_key = cum.reshape(batch, seq, FOX_HEADS)
    f_query = cum_t.reshape(batch, groups, nh, seq)
    grid_spec = pltpu.PrefetchScalarGridSpec(
        num_scalar_prefetch=2,
        grid=(batch, groups, len(pairs)),
        in_specs=[
            pl.BlockSpec((tq, w), lambda b, h, p, qi, kj: (b * nq + qi[p], h)),
            pl.BlockSpec((tq, w), lambda b, h, p, qi, kj: (b * nq + kj[p], h)),
            pl.BlockSpec((None, nh * LANES, tq), lambda b, h, p, qi, kj: (b, h, kj[p])),
            pl.BlockSpec((None, None, nh, tq), lambda b, h, p, qi, kj: (b, h, 0, qi[p])),
            pl.BlockSpec((None, tq, FOX_HEADS), lambda b, h, p, qi, kj: (b, kj[p], 0)),
        ],
        out_specs=pl.BlockSpec((tq, w), lambda b, h, p, qi, kj: (b * nq + qi[p], h)),
        scratch_shapes=[
            pltpu.VMEM((nh, tq, LANES), BF),
            pltpu.VMEM((nh, 1, tq), F32),
            pltpu.VMEM((nh, LANES, tq), F32),
        ],
    )
    return pl.pallas_call(
        _fox_prompt_kernel,
        grid_spec=grid_spec,
        out_shape=jax.ShapeDtypeStruct((batch * seq, D_MODEL), BF),
        compiler_params=_cparams("parallel", "parallel", "arbitrary"),
        name="fox_prompt_attention",
    )(qi_tab, kj_tab, q, k, vo, f_query, f_key)


def _fox_sample_kernel(pt_ref, q_ref, kn_ref, vn_ref, lfn_ref, *refs, npg, seq):
    kt_refs = refs[:npg]
    vt_refs = refs[npg:2 * npg]
    lf_refs = refs[2 * npg:3 * npg]
    o_ref = refs[3 * npg]
    qbd_ref, carry_ref, fn_ref, m_ref, l_ref, acc_ref, kpad_ref, vpad_ref = refs[3 * npg + 1:]
    g = pl.program_id(1)
    C = seq * FOX_HEADS
    P = kpad_ref.shape[0]
    row = lax.broadcasted_iota(jnp.int32, (C, D_MODEL), 0)
    col = lax.broadcasted_iota(jnp.int32, (C, D_MODEL), 1)
    own_head = (col >> 6) == (row % FOX_HEADS)
    t_of = lax.broadcasted_iota(jnp.int32, (C, P), 0) // FOX_HEADS
    j_of = lax.broadcasted_iota(jnp.int32, (C, P), 1)
    kk = lax.broadcasted_iota(jnp.int32, (P, P), 0)
    jj = lax.broadcasted_iota(jnp.int32, (P, P), 1)

    def tile_heads(a):
        return jnp.concatenate([a] * seq, axis=0)

    @pl.when(g == 0)
    def _():
        q = q_ref[...]
        qrep = jnp.concatenate(
            [jnp.broadcast_to(q[t:t + 1, :], (FOX_HEADS, D_MODEL)) for t in range(seq)], axis=0)
        qbd = jnp.where(own_head, qrep, 0.0).astype(BF)
        qbd_ref[...] = qbd
        kpad_ref[...] = jnp.zeros_like(kpad_ref)
        vpad_ref[...] = jnp.zeros_like(vpad_ref)
        kpad_ref[0:seq, :] = kn_ref[...]
        vpad_ref[0:seq, :] = vn_ref[...]
        fn = _dot_exact(lfn_ref[...], jnp.where(kk <= jj, 1.0, 0.0))
        fn_c = tile_heads(fn)
        fn_col = jnp.sum(jnp.where(j_of == t_of, fn_c, 0.0), axis=-1, keepdims=True)
        fn_ref[...] = fn_col
        s = lax.dot_general(qbd, kpad_ref[...].astype(BF), _NT, preferred_element_type=F32)
        s = jnp.where(j_of <= t_of, s + fn_col - fn_c, NEG)
        m = jnp.max(s, axis=-1, keepdims=True)
        pm = jnp.exp(s - m)
        m_ref[...] = m
        l_ref[...] = jnp.sum(pm, axis=-1, keepdims=True)
        acc_ref[...] = _dot(pm.astype(BF), vpad_ref[...].astype(BF))
        carry_ref[...] = jnp.zeros_like(carry_ref)

    later = jnp.where(kk > jj, 1.0, 0.0)
    qbd = qbd_ref[...]
    fn_col = fn_ref[...]
    carry = carry_ref[...]
    scores = []
    for i in reversed(range(npg)):
        lf = lf_refs[i][...]
        r_page = _dot_exact(lf, later) + carry
        carry = carry + jnp.sum(lf, axis=-1, keepdims=True)
        s = _dot(qbd, kt_refs[i][...].astype(BF))
        scores.append((i, s + fn_col + tile_heads(r_page)))
    carry_ref[...] = carry
    m_old = m_ref[...]
    m_new = m_old
    for _, s in scores:
        m_new = jnp.maximum(m_new, jnp.max(s, axis=-1, keepdims=True))
    alpha = jnp.exp(m_old - m_new)
    l_new = alpha * l_ref[...]
    acc = alpha * acc_ref[...]
    for i, s in scores:
        pm = jnp.exp(s - m_new)
        l_new = l_new + jnp.sum(pm, axis=-1, keepdims=True)
        acc = acc + lax.dot_general(pm.astype(BF), vt_refs[i][...].astype(BF), _NT,
                                    preferred_element_type=F32)
    m_ref[...] = m_new
    l_ref[...] = l_new
    acc_ref[...] = acc

    @pl.when(g == pl.num_programs(1) - 1)
    def _():
        on = jnp.where(own_head, acc / l_new, 0.0)
        for t in range(seq):
            o_ref[t:t + 1, :] = jnp.sum(on[t * FOX_HEADS:(t + 1) * FOX_HEADS, :], axis=0, keepdims=True)


def _fox_sample(q, k_new, v_new, lf_new_t, cache_kt, cache_vt, cache_lf_t, page_table, seq, npg):
    batch, n_pages = page_table.shape
    P = cache_kt.shape[-1]
    groups = n_pages // npg
    C = seq * FOX_HEADS

    def page(i):
        return lambda b, g, pt: (pt[b, (groups - 1 - g) * npg + i], 0, 0)

    new_tok = pl.BlockSpec((None, seq, D_MODEL), lambda b, g, pt: (b, 0, 0))
    grid_spec = pltpu.PrefetchScalarGridSpec(
        num_scalar_prefetch=1,
        grid=(batch, groups),
        in_specs=(
            [new_tok, new_tok, new_tok,
             pl.BlockSpec((None, FOX_HEADS, P), lambda b, g, pt: (b, 0, 0))]
            + [pl.BlockSpec((None, D_MODEL, P), page(i)) for i in range(npg)]
            + [pl.BlockSpec((None, D_MODEL, P), page(i)) for i in range(npg)]
            + [pl.BlockSpec((None, FOX_HEADS, P), page(i)) for i in range(npg)]
        ),
        out_specs=new_tok,
        scratch_shapes=[
            pltpu.VMEM((C, D_MODEL), BF),
            pltpu.VMEM((FOX_HEADS, 1), F32),
            pltpu.VMEM((C, 1), F32),
            pltpu.VMEM((C, 1), F32),
            pltpu.VMEM((C, 1), F32),
            pltpu.VMEM((C, D_MODEL), F32),
            pltpu.VMEM((P, D_MODEL), F32),
            pltpu.VMEM((P, D_MODEL), F32),
        ],
    )
    return pl.pallas_call(
        functools.partial(_fox_sample_kernel, npg=npg, seq=seq),
        grid_spec=grid_spec,
        out_shape=jax.ShapeDtypeStruct((batch, seq, D_MODEL), F32),
        compiler_params=_cparams("parallel", "arbitrary"),
        name="fox_sample_attention",
    )(page_table, q, k_new, v_new, lf_new_t,
      *([cache_kt] * npg), *([cache_vt] * npg), *([cache_lf_t] * npg))


def _prep_weights(w_ret_in, w_ret_out, w_q, w_o, w_ffn_in, w_ffn_out, w_kvf):
    depth, _, two_f = w_ffn_in.shape
    d_ff = two_f // 2
    nf = d_ff // FFN_CHUNK
    wi = w_ffn_in.astype(BF).reshape(depth, D_MODEL, 2 * nf, FFN_CHUNK).transpose(0, 2, 1, 3)
    wo = w_ffn_out.astype(BF).reshape(depth, nf, FFN_CHUNK, D_MODEL)
    w_kvf_b = w_kvf.astype(BF)
    wf = jnp.zeros((D_MODEL, LANES), BF).at[:, :FOX_HEADS].set(w_kvf_b[:, 2 * D_MODEL:])
    return dict(
        ret_in=w_ret_in.astype(BF), ret_out=w_ret_out.astype(BF), q=w_q.astype(BF), o=w_o.astype(BF),
        ffn_in=wi, ffn_out=wo, wk=w_kvf_b[:, :D_MODEL], wv=w_kvf_b[:, D_MODEL:2 * D_MODEL], wf=wf)


def _rope_tables(pos):
    half = RET_DK // 2
    freqs = ROPE_BASE ** (-jnp.arange(half, dtype=F32) / half)
    ang = pos.astype(F32)[:, None] * freqs[None, :]
    return jnp.cos(ang), jnp.sin(ang)


def _trunk(x, mods, kvmod, mod, pos, W, norms, b_f, *, tm, kvf_seq, ret_fn, attn_fn):
    ret_norm_g, attn_norm_g, q_norm_g, ffn_norm_g, kv_norm_g, k_norm_g = norms
    n_a = W["ret_in"].shape[0]
    depth = W["ffn_in"].shape[0]
    log_g = jnp.log1p(-jnp.exp2(-5.0 - jnp.arange(RET_HEADS, dtype=F32)))
    cos, sin = _rope_tables(pos)
    states = []
    shared = None
    for l in range(depth):
        sh1, sc1, g1, sh2, sc2, g2 = mods[l]
        if l < n_a:
            a, S = ret_fn(l, x, ret_norm_g[l], sh1, sc1, cos, sin, log_g)
            states.append(S)
            x = _out_proj(a, W["ret_out"], l, x, g1, mod, tm)
        else:
            j = l - n_a
            if shared is None:
                shared = _kvf_proj(x, kv_norm_g, kvmod[0], kvmod[1], W["wk"], W["wv"], W["wf"],
                                   b_f, k_norm_g, mod, min(tm, 512), kvf_seq)
            a = attn_fn(j, x, attn_norm_g[j], sh1, sc1, q_norm_g[j], shared)
            x = _out_proj(a, W["o"], j, x, g1, mod, tm)
        x = _ffn(x, ffn_norm_g[l], sh2, sc2, g2, W["ffn_in"], W["ffn_out"], l, mod, tm)
    return x, jnp.stack(states), shared


def kernel(x_prompt, x_sample, cache_k, cache_v, cache_logf, state_ret, page_table, c_prompt, c_sample,
           w_mod, b_mod, ret_norm_g, w_ret_in, w_ret_out, attn_norm_g, w_q, q_norm_g, w_o, ffn_norm_g,
           w_ffn_in, w_ffn_out, kv_norm_g, w_kvmod, b_kvmod, w_kvf, b_f, k_norm_g):
    Bp, Tp, _ = x_prompt.shape
    Bd, Td, _ = x_sample.shape
    depth = w_mod.shape[0]
    n_pool, page_size = cache_k.shape[:2]
    past_len = page_table.shape[1] * page_size
    W = _prep_weights(w_ret_in, w_ret_out, w_q, w_o, w_ffn_in, w_ffn_out, w_kvf)
    norms = (ret_norm_g, attn_norm_g, q_norm_g, ffn_norm_g, kv_norm_g, k_norm_g)

    c_all = jnp.concatenate([c_prompt, c_sample], axis=0)
    mod_all = _cond_proj(c_all, w_mod, b_mod, 1536)
    kvmod_all = _cond_proj(c_all, w_kvmod[None], b_kvmod[None], 1024)[0]

    def split_mods(lo, hi, expand):
        def lay(a):
            return jnp.repeat(a, expand, axis=0)[None] if expand else a[:, None, :]
        mods = [[lay(m) for m in jnp.split(mod_all[l, lo:hi], 6, axis=-1)] for l in range(depth)]
        kvmod = [lay(m) for m in jnp.split(kvmod_all[lo:hi], 2, axis=-1)]
        return mods, kvmod

    mods_p, kvmod_p = split_mods(0, Bp, 0)
    mod_p = _Mod(Tp, per_row=False)
    tm_p = 1024

    def ret_prompt(l, x, g, sh, sc, cos, sin, log_g):
        proj = _nm_matmul(x, g, sh, sc, W["ret_in"], l, mod_p, tm_p, 3072, BF)
        return _ret_prompt(proj, cos, sin, log_g, Bp, Tp, 2048)

    def attn_prompt(j, x, g, sh, sc, gq, shared):
        _, _, _, cum_t, cum, kb, vo = shared
        q = _q_proj(x, g, sh, sc, W["q"], j, gq, mod_p, tm_p, BF, FOX_HD ** -0.5 * LOG2E)
        return _fox_prompt(q, kb, vo, cum, cum_t, Bp, Tp, 1024, 8)

    y_p, st_p, (kt_p, vt_p, lft_p, _, _, _, _) = _trunk(
        x_prompt.reshape(Bp * Tp, D_MODEL), mods_p, kvmod_p, mod_p, jnp.arange(Tp), W, norms, b_f,
        tm=tm_p, kvf_seq=Tp, ret_fn=ret_prompt, attn_fn=attn_prompt)

    mods_s, kvmod_s = split_mods(Bp, Bp + Bd, Td)
    Ms = Bd * Td
    mod_s = _Mod(Ms, per_row=True)
    ckt = cache_k.transpose(0, 2, 3, 1).reshape(n_pool, D_MODEL, page_size)
    cvt = cache_v.transpose(0, 2, 3, 1).reshape(n_pool, D_MODEL, page_size)
    clf_t = cache_logf.transpose(0, 2, 1)

    def ret_sample(l, x, g, sh, sc, cos, sin, log_g):
        proj = _nm_matmul(x, g, sh, sc, W["ret_in"], l, mod_s, Ms, 1536, F32)
        return _ret_sample(proj, state_ret, l, cos, sin, log_g, Td, 2)

    def attn_sample(j, x, g, sh, sc, gq, shared):
        k, v, lf = shared
        q = _q_proj(x, g, sh, sc, W["q"], j, gq, mod_s, Ms, F32, FOX_HD ** -0.5)
        lf_t = jnp.pad(lf.reshape(Bd, Td, FOX_HEADS).transpose(0, 2, 1),
                       ((0, 0), (0, 0), (0, page_size - Td)))
        o = _fox_sample(q.reshape(Bd, Td, D_MODEL), k.reshape(Bd, Td, D_MODEL), v.reshape(Bd, Td, D_MODEL),
                        lf_t, ckt, cvt, clf_t, page_table, Td, 16)
        return o.reshape(Ms, D_MODEL)

    y_s, st_s, (k_s, v_s, lf_s) = _trunk(
        x_sample.reshape(Ms, D_MODEL), mods_s, kvmod_s, mod_s, past_len + jnp.arange(Td), W, norms, b_f,
        tm=Ms, kvf_seq=None, ret_fn=ret_sample, attn_fn=attn_sample)

    heads = (FOX_HEADS, FOX_HD)
    k_p = kt_p.reshape(Bp, *heads, Tp).transpose(0, 3, 1, 2)
    v_p = vt_p.reshape(Bp, *heads, Tp).transpose(0, 3, 1, 2)
    return (y_p.reshape(Bp, Tp, D_MODEL), y_s.reshape(Bd, Td, D_MODEL), st_p, st_s,
            k_p, v_p, lft_p.transpose(0, 2, 1),
            k_s.reshape(Bd, Td, *heads), v_s.reshape(Bd, Td, *heads), lf_s.reshape(Bd, Td, FOX_HEADS))
```

```python
import functools
import math

import jax
import jax.numpy as jnp
from jax import lax
from jax.experimental import pallas as pl
from jax.experimental.pallas import tpu as pltpu

BF = jnp.bfloat16
F32 = jnp.float32

D_MODEL = 1024
RET_HEADS = 4
RET_DK = 256
RET_DV = 512
RET_CHUNK = 128
ROPE_BASE = 10000.0
FOX_HEADS = 16
FOX_HD = 64
EPS = 1e-6
LANES = 128
VMEM_LIMIT = 52 * 1024 * 1024
NEG = -1e30
LOG2E = math.log2(math.e)

_NT = (((1,), (1,)), ((), ()))
_TN = (((0,), (0,)), ((), ()))


def _cparams(*sem):
    return pltpu.CompilerParams(dimension_semantics=sem, vmem_limit_bytes=VMEM_LIMIT)


def _dot(a, b):
    return jnp.dot(a, b, preferred_element_type=F32)


def _dot_exact(a, b):
    return jnp.dot(a, b, preferred_element_type=F32, precision=lax.Precision.HIGHEST)


def _silu(x):
    return x * jax.nn.sigmoid(x)


def _normmod(x, g, shift, scale):
    ms = jnp.mean(x * x, axis=-1, keepdims=True)
    y = x * lax.rsqrt(ms + EPS) * g
    return y * (1.0 + scale) + shift


def _split_bf16(x):
    hi = x.astype(BF)
    lo = (x - hi.astype(F32)).astype(BF)
    return hi, lo


def _headnorm(a):
    r = lax.broadcasted_iota(jnp.int32, (D_MODEL, LANES), 0) >> 6
    c = lax.broadcasted_iota(jnp.int32, (D_MODEL, LANES), 1)
    seg = jnp.where(r == c, 1.0, 0.0).astype(BF)
    rt = lax.broadcasted_iota(jnp.int32, (LANES, D_MODEL), 0)
    ct = lax.broadcasted_iota(jnp.int32, (LANES, D_MODEL), 1) >> 6
    seg_t = jnp.where(rt == ct, 1.0, 0.0).astype(BF)
    y_hi, y_lo = _split_bf16(a * a)
    ms = (_dot(y_hi, seg) + _dot(y_lo, seg)) * (1.0 / FOX_HD)
    rs = lax.rsqrt(ms + EPS)
    r_hi, r_lo = _split_bf16(rs)
    return a * (_dot(r_hi, seg_t) + _dot(r_lo, seg_t))


def _log_sigmoid(x):
    return jnp.minimum(x, 0.0) - jnp.log1p(jnp.exp(-jnp.abs(x)))


def _cond_kernel(c_ref, w_ref, b_ref, o_ref):
    ca = _silu(c_ref[...]).astype(BF)
    o_ref[...] = _dot(ca, w_ref[...].astype(BF)) + b_ref[...]


def _cond_proj(c, w, b, tn):
    L, K, N = w.shape
    R = c.shape[0]
    return pl.pallas_call(
        _cond_kernel,
        grid=(L, N // tn),
        in_specs=[
            pl.BlockSpec((R, K), lambda l, j: (0, 0)),
            pl.BlockSpec((None, K, tn), lambda l, j: (l, 0, j)),
            pl.BlockSpec((None, 1, tn), lambda l, j: (l, 0, j)),
        ],
        out_specs=pl.BlockSpec((None, R, tn), lambda l, j: (l, 0, j)),
        out_shape=jax.ShapeDtypeStruct((L, R, N), F32),
        compiler_params=_cparams("parallel", "parallel"),
        name="cond_proj",
    )(c, w, b.reshape(L, 1, N))


class _Mod:
    def __init__(self, rows_per_group, per_row):
        self.rows_per_group = rows_per_group
        self.per_row = per_row

    def spec(self, tm):
        if self.per_row:
            return pl.BlockSpec((None, tm, D_MODEL), lambda i, *_: (0, i, 0))
        tiles = self.rows_per_group // tm
        return pl.BlockSpec((None, 1, D_MODEL), lambda i, *_: (i // tiles, 0, 0))


def _vec_spec(n):
    return pl.BlockSpec((1, n), lambda i, *_: (0, 0))


def _nm_kernel(x_ref, g_ref, sh_ref, sc_ref, w_ref, o_ref, h_ref):
    @pl.when(pl.program_id(1) == 0)
    def _():
        h_ref[...] = _normmod(x_ref[...], g_ref[...], sh_ref[...], sc_ref[...]).astype(BF)

    o_ref[...] = _dot(h_ref[...], w_ref[...]).astype(o_ref.dtype)


def _nm_matmul(x, g, shift, scale, w, layer, mod, tm, tn, out_dtype):
    M, K = x.shape
    N = w.shape[2]
    return pl.pallas_call(
        _nm_kernel,
        grid=(M // tm, N // tn),
        in_specs=[
            pl.BlockSpec((tm, K), lambda i, j: (i, 0)),
            _vec_spec(K),
            mod.spec(tm),
            mod.spec(tm),
            pl.BlockSpec((None, K, tn), lambda i, j: (layer, 0, j)),
        ],
        out_specs=pl.BlockSpec((tm, tn), lambda i, j: (i, j)),
        out_shape=jax.ShapeDtypeStruct((M, N), out_dtype),
        scratch_shapes=[pltpu.VMEM((tm, K), BF)],
        compiler_params=_cparams("parallel", "arbitrary"),
        name="norm_mod_matmul",
    )(x, g.reshape(1, K), shift, scale, w)


def _q_kernel(x_ref, g_ref, sh_ref, sc_ref, w_ref, gq_ref, o_ref, *, q_scale):
    h = _normmod(x_ref[...], g_ref[...], sh_ref[...], sc_ref[...]).astype(BF)
    q = _headnorm(_dot(h, w_ref[...])) * gq_ref[...]
    o_ref[...] = (q * q_scale).astype(o_ref.dtype)


def _q_proj(x, g, shift, scale, w, layer, gq, mod, tm, out_dtype, q_scale):
    M, K = x.shape
    return pl.pallas_call(
        functools.partial(_q_kernel, q_scale=q_scale),
        grid=(M // tm,),
        in_specs=[
            pl.BlockSpec((tm, K), lambda i: (i, 0)),
            _vec_spec(K),
            mod.spec(tm),
            mod.spec(tm),
            pl.BlockSpec((None, K, D_MODEL), lambda i: (layer, 0, 0)),
            _vec_spec(D_MODEL),
        ],
        out_specs=pl.BlockSpec((tm, D_MODEL), lambda i: (i, 0)),
        out_shape=jax.ShapeDtypeStruct((M, D_MODEL), out_dtype),
        compiler_params=_cparams("parallel"),
        name="q_proj_headnorm",
    )(x, g.reshape(1, K), shift, scale, w, jnp.tile(gq, FOX_HEADS).reshape(1, D_MODEL))


def _kvf_core(x_ref, g_ref, sh_ref, sc_ref, wk_ref, wv_ref, wf_ref, bf_ref, gk_ref):
    h = _normmod(x_ref[...], g_ref[...], sh_ref[...], sc_ref[...]).astype(BF)
    k = _headnorm(_dot(h, wk_ref[...])) * gk_ref[...]
    v = _dot(h, wv_ref[...])
    lf = _log_sigmoid(_dot(h, wf_ref[...]) + bf_ref[...])
    return k, v, lf


def _kvf_rows_kernel(*refs):
    k_ref, v_ref, lf_ref = refs[9:]
    k, v, lf = _kvf_core(*refs[:9])
    k_ref[...] = k
    v_ref[...] = v
    lf_ref[...] = lf[:, :FOX_HEADS]


def _kvf_seq_kernel(*refs, tiles_per_seq):
    kt_ref, vt_ref, lft_ref, cumt_ref, cum_ref, kb_ref, vto_ref, carry_ref = refs[9:]
    tm = kb_ref.shape[0]
    k, v, lf = _kvf_core(*refs[:9])
    kt_ref[...] = k.T
    vt = v.T
    vt_ref[...] = vt
    vto_ref[...] = jnp.ones_like(vto_ref)
    for h in range(FOX_HEADS):
        dst = h * LANES + (h % 2) * FOX_HD
        vto_ref[dst:dst + FOX_HD, :] = vt[h * FOX_HD:(h + 1) * FOX_HD, :].astype(BF)
    kb_ref[...] = k.astype(BF)
    lft_ref[...] = lf.T[:FOX_HEADS, :]

    @pl.when(pl.program_id(0) % tiles_per_seq == 0)
    def _():
        carry_ref[...] = jnp.zeros_like(carry_ref)

    r = lax.broadcasted_iota(jnp.int32, (tm, tm), 0)
    c = lax.broadcasted_iota(jnp.int32, (tm, tm), 1)
    tri = jnp.where(r >= c, 1.0, 0.0)
    cum = _dot_exact(tri, lf) + carry_ref[...]
    carry_ref[...] = cum[tm - 1:tm, :]
    cum_ref[...] = cum[:, :FOX_HEADS]
    cumt_ref[...] = cum.T[:FOX_HEADS, :]


def _kvf_proj(x, g, shift, scale, wk, wv, wf, b_f, gk, mod, tm, seq_len=None):
    M, K = x.shape
    tok = pl.BlockSpec((tm, D_MODEL), lambda i: (i, 0))
    hd = pl.BlockSpec((tm, FOX_HEADS), lambda i: (i, 0))
    w_spec = pl.BlockSpec((K, D_MODEL), lambda i: (0, 0))
    bf_pad = jnp.zeros((1, LANES), F32).at[0, :FOX_HEADS].set(b_f)
    in_specs = [
        pl.BlockSpec((tm, K), lambda i: (i, 0)),
        _vec_spec(K),
        mod.spec(tm),
        mod.spec(tm),
        w_spec,
        w_spec,
        pl.BlockSpec((K, LANES), lambda i: (0, 0)),
        _vec_spec(LANES),
        _vec_spec(D_MODEL),
    ]
    args = (x, g.reshape(1, K), shift, scale, wk, wv, wf, bf_pad, jnp.tile(gk, FOX_HEADS).reshape(1, D_MODEL))
    if seq_len is None:
        return pl.pallas_call(
            _kvf_rows_kernel,
            grid=(M // tm,),
            in_specs=in_specs,
            out_specs=[tok, tok, hd],
            out_shape=[
                jax.ShapeDtypeStruct((M, D_MODEL), F32),
                jax.ShapeDtypeStruct((M, D_MODEL), F32),
                jax.ShapeDtypeStruct((M, FOX_HEADS), F32),
            ],
            compiler_params=_cparams("parallel"),
            name="shared_kvf_rows",
        )(*args)
    tiles = seq_len // tm
    batch = M // seq_len
    feat_t = pl.BlockSpec((None, D_MODEL, tm), lambda i: (i // tiles, 0, i % tiles))
    head_t = pl.BlockSpec((None, FOX_HEADS, tm), lambda i: (i // tiles, 0, i % tiles))
    return pl.pallas_call(
        functools.partial(_kvf_seq_kernel, tiles_per_seq=tiles),
        grid=(M // tm,),
        in_specs=in_specs,
        out_specs=[feat_t, feat_t, head_t, head_t, hd, tok,
                   pl.BlockSpec((None, FOX_HEADS * LANES, tm), lambda i: (i // tiles, 0, i % tiles))],
        out_shape=[
            jax.ShapeDtypeStruct((batch, D_MODEL, seq_len), F32),
            jax.ShapeDtypeStruct((batch, D_MODEL, seq_len), F32),
            jax.ShapeDtypeStruct((batch, FOX_HEADS, seq_len), F32),
            jax.ShapeDtypeStruct((batch, FOX_HEADS, seq_len), F32),
            jax.ShapeDtypeStruct((M, FOX_HEADS), F32),
            jax.ShapeDtypeStruct((M, D_MODEL), BF),
            jax.ShapeDtypeStruct((batch, FOX_HEADS * LANES, seq_len), BF),
        ],
        scratch_shapes=[pltpu.VMEM((1, LANES), F32)],
        compiler_params=_cparams("arbitrary"),
        name="shared_kvf_seq",
    )(*args)


def _out_kernel(a_ref, w_ref, x_ref, gt_ref, o_ref):
    y = _dot(a_ref[...].astype(BF), w_ref[...])
    o_ref[...] = x_ref[...] + (1.0 + gt_ref[...]) * y


def _out_proj(a, w, layer, x, gate, mod, tm):
    M, K = a.shape
    return pl.pallas_call(
        _out_kernel,
        grid=(M // tm,),
        in_specs=[
            pl.BlockSpec((tm, K), lambda i: (i, 0)),
            pl.BlockSpec((None, K, D_MODEL), lambda i: (layer, 0, 0)),
            pl.BlockSpec((tm, D_MODEL), lambda i: (i, 0)),
            mod.spec(tm),
        ],
        out_specs=pl.BlockSpec((tm, D_MODEL), lambda i: (i, 0)),
        out_shape=jax.ShapeDtypeStruct((M, D_MODEL), F32),
        compiler_params=_cparams("parallel"),
        name="out_proj_residual",
    )(a, w, x, gate)


FFN_CHUNK = 256


def _ffn_kernel(x_ref, g_ref, sh_ref, sc_ref, gt_ref, wi_ref, wo_ref, o_ref, h_ref, acc_ref, *, nf):
    h_ref[...] = _normmod(x_ref[...], g_ref[...], sh_ref[...], sc_ref[...]).astype(BF)
    acc_ref[...] = jnp.zeros_like(acc_ref)

    def body(f, carry):
        h = h_ref[...]
        gate = _dot(h, wi_ref[f])
        up = _dot(h, wi_ref[nf + f])
        act = (_silu(gate) * up).astype(BF)
        acc_ref[...] += _dot(act, wo_ref[f])
        return carry

    lax.fori_loop(0, nf, body, 0)
    o_ref[...] = x_ref[...] + (1.0 + gt_ref[...]) * acc_ref[...]


def _ffn(x, g, shift, scale, gate, wi, wo, layer, mod, tm):
    M, K = x.shape
    nf = wo.shape[1]
    resident = pl.Buffered(1)
    return pl.pallas_call(
        functools.partial(_ffn_kernel, nf=nf),
        grid=(M // tm,),
        in_specs=[
            pl.BlockSpec((tm, K), lambda i: (i, 0)),
            _vec_spec(K),
            mod.spec(tm),
            mod.spec(tm),
            mod.spec(tm),
            pl.BlockSpec((None,) + wi.shape[1:], lambda i: (layer, 0, 0, 0), pipeline_mode=resident),
            pl.BlockSpec((None,) + wo.shape[1:], lambda i: (layer, 0, 0, 0), pipeline_mode=resident),
        ],
        out_specs=pl.BlockSpec((tm, K), lambda i: (i, 0)),
        out_shape=jax.ShapeDtypeStruct((M, K), F32),
        scratch_shapes=[pltpu.VMEM((tm, K), BF), pltpu.VMEM((tm, K), F32)],
        compiler_params=_cparams("parallel"),
        name="ffn_swiglu",
    )(x, g.reshape(1, K), shift, scale, gate, wi, wo)


def _rotary(x, cos, sin):
    half = x.shape[-1] // 2
    x1, x2 = x[:, :half], x[:, half:]
    return jnp.concatenate([x1 * cos - x2 * sin, x1 * sin + x2 * cos], axis=-1)


def _ret_prompt_kernel(q_ref, k_ref, v_ref, g_ref, cos_ref, sin_ref, lg_ref, o_ref, s_ref, *, rows):
    C = RET_CHUNK

    @pl.when(pl.program_id(2) == 0)
    def _():
        s_ref[...] = jnp.zeros_like(s_ref)

    lg = lg_ref[...]
    i_v = lax.broadcasted_iota(jnp.int32, (C, RET_DV), 0).astype(F32)
    i_k = lax.broadcasted_iota(jnp.int32, (C, RET_DK), 0).astype(F32)
    cross = jnp.exp(lg * (i_v + 1.0))
    kdec = jnp.exp(lg_ref[:, :RET_DK] * (C - 1.0 - i_k))
    cdec = jnp.exp(lg * float(C))
    ii = lax.broadcasted_iota(jnp.int32, (C, C), 0)
    jj = lax.broadcasted_iota(jnp.int32, (C, C), 1)
    diff = (ii - jj).astype(F32)
    inner = jnp.where(diff >= 0, jnp.exp(lg_ref[:, :C] * jnp.maximum(diff, 0.0)), 0.0)

    for c in range(rows // C):
        sl = slice(c * C, (c + 1) * C)
        cos, sin = cos_ref[sl, :], sin_ref[sl, :]
        q = _rotary(q_ref[sl, :].astype(F32), cos, sin)
        k = _rotary(k_ref[sl, :].astype(F32), cos, sin) * (RET_DK ** -0.5)
        v = v_ref[sl, :]
        qb = q.astype(BF)
        S = s_ref[...]
        sc = lax.dot_general(qb, k.astype(BF), _NT, preferred_element_type=F32) * inner
        o = _dot(sc.astype(BF), v) + _dot(qb, S.astype(BF)) * cross
        s_ref[...] = S * cdec + lax.dot_general((k * kdec).astype(BF), v, _TN,
                                                preferred_element_type=F32)
        o = o * lax.rsqrt(jnp.mean(o * o, axis=-1, keepdims=True) + EPS)
        o_ref[sl, :] = (o * _silu(g_ref[sl, :].astype(F32))).astype(o_ref.dtype)


def _ret_prompt(proj, cos, sin, log_g, batch, seq, rows):
    nr = seq // rows
    kq = D_MODEL // RET_DK
    kv = 2 * D_MODEL // RET_DV
    kg = 4 * D_MODEL // RET_DV
    row = lambda b, h, r: b * nr + r
    lg = jnp.broadcast_to(log_g[:, None, None], (RET_HEADS, 1, RET_DV))
    return pl.pallas_call(
        functools.partial(_ret_prompt_kernel, rows=rows),
        grid=(batch, RET_HEADS, nr),
        in_specs=[
            pl.BlockSpec((rows, RET_DK), lambda b, h, r: (row(b, h, r), h)),
            pl.BlockSpec((rows, RET_DK), lambda b, h, r: (row(b, h, r), kq + h)),
            pl.BlockSpec((rows, RET_DV), lambda b, h, r: (row(b, h, r), kv + h)),
            pl.BlockSpec((rows, RET_DV), lambda b, h, r: (row(b, h, r), kg + h)),
            pl.BlockSpec((rows, RET_DK // 2), lambda b, h, r: (r, 0)),
            pl.BlockSpec((rows, RET_DK // 2), lambda b, h, r: (r, 0)),
            pl.BlockSpec((None, 1, RET_DV), lambda b, h, r: (h, 0, 0)),
        ],
        out_specs=[
            pl.BlockSpec((rows, RET_DV), lambda b, h, r: (row(b, h, r), h)),
            pl.BlockSpec((None, None, RET_DK, RET_DV), lambda b, h, r: (b, h, 0, 0)),
        ],
        out_shape=[
            jax.ShapeDtypeStruct((batch * seq, RET_HEADS * RET_DV), BF),
            jax.ShapeDtypeStruct((batch, RET_HEADS, RET_DK, RET_DV), F32),
        ],
        compiler_params=_cparams("parallel", "parallel", "arbitrary"),
        name="retention_prompt",
    )(proj, proj, proj, proj, cos, sin, lg)


def _ret_sample_kernel(p_ref, s0_ref, cos_ref, sin_ref, lg_ref, *rest, nb, seq, layer):
    if layer:
        prev_ref, o_ref, s_ref = rest
        s_ref[0:layer] = prev_ref[...]
    else:
        o_ref, s_ref = rest
    R = nb * seq
    ri = lax.broadcasted_iota(jnp.int32, (R, R), 0)
    rj = lax.broadcasted_iota(jnp.int32, (R, R), 1)
    same = (ri // seq) == (rj // seq)
    diff = (ri - rj).astype(F32)
    t_row = (lax.broadcasted_iota(jnp.int32, (R, 1), 0) % seq).astype(F32)
    b_row = lax.broadcasted_iota(jnp.int32, (R, 1), 0) // seq
    cos, sin = cos_ref[...], sin_ref[...]
    for h in range(RET_HEADS):
        lg = lg_ref[h]
        inner = jnp.where(same & (diff >= 0), jnp.exp(lg[:, :R] * jnp.maximum(diff, 0.0)), 0.0)
        cross = jnp.exp(lg * (t_row + 1.0))
        kdec = jnp.exp(lg[:, :RET_DK] * (seq - 1.0 - t_row))
        cdec = jnp.exp(lg * float(seq))
        q = _rotary(p_ref[:, h * RET_DK:(h + 1) * RET_DK], cos, sin)
        k = _rotary(p_ref[:, D_MODEL + h * RET_DK:D_MODEL + (h + 1) * RET_DK], cos, sin)
        k = k * (RET_DK ** -0.5)
        v = p_ref[:, 2 * D_MODEL + h * RET_DV:2 * D_MODEL + (h + 1) * RET_DV]
        g = p_ref[:, 4 * D_MODEL + h * RET_DV:4 * D_MODEL + (h + 1) * RET_DV]
        qb, vb = q.astype(BF), v.astype(BF)
        sc = lax.dot_general(qb, k.astype(BF), _NT, preferred_element_type=F32) * inner
        o = _dot(sc.astype(BF), vb)
        kd = k * kdec
        for b in range(nb):
            S = s0_ref[b, h]
            mine = b_row == b
            o = o + jnp.where(mine, _dot(qb, S.astype(BF)) * cross, 0.0)
            kb = jnp.where(mine, kd, 0.0).astype(BF)
            s_ref[layer, b, h] = S * cdec + lax.dot_general(kb, vb, _TN, preferred_element_type=F32)
        o = o * lax.rsqrt(jnp.mean(o * o, axis=-1, keepdims=True) + EPS)
        o_ref[:, h * RET_DV:(h + 1) * RET_DV] = o * _silu(g)


def _ret_sample(proj, s0, layer, prev, cos, sin, log_g, seq, nb):
    M = proj.shape[0]
    batch = M // seq
    R = nb * seq
    lg = jnp.broadcast_to(log_g[:, None, None], (RET_HEADS, 1, RET_DV))
    stacked = lambda n: pl.BlockSpec((n, nb, RET_HEADS, RET_DK, RET_DV), lambda i: (0, i, 0, 0, 0))
    state_in = pl.BlockSpec((None, nb, RET_HEADS, RET_DK, RET_DV), lambda i: (layer, i, 0, 0, 0))
    return pl.pallas_call(
        functools.partial(_ret_sample_kernel, nb=nb, seq=seq, layer=layer),
        grid=(batch // nb,),
        in_specs=[
            pl.BlockSpec((R, 6 * D_MODEL), lambda i: (i, 0)),
            state_in,
            pl.BlockSpec((R, RET_DK // 2), lambda i: (0, 0)),
            pl.BlockSpec((R, RET_DK // 2), lambda i: (0, 0)),
            pl.BlockSpec((RET_HEADS, 1, RET_DV), lambda i: (0, 0, 0)),
        ] + ([stacked(layer)] if layer else []),
        out_specs=[pl.BlockSpec((R, RET_HEADS * RET_DV), lambda i: (i, 0)), stacked(layer + 1)],
        out_shape=[
            jax.ShapeDtypeStruct((M, RET_HEADS * RET_DV), F32),
            jax.ShapeDtypeStruct((layer + 1,) + s0.shape[1:], F32),
        ],
        compiler_params=_cparams("parallel"),
        name="retention_sample",
    )(proj, s0, jnp.tile(cos, (nb, 1)), jnp.tile(sin, (nb, 1)), lg, *([prev] if layer else []))


def _fox_prompt_kernel(qi_ref, kj_ref, q_ref, k_ref, vo_ref, fq_ref, fk_ref, o_ref,
                       qm_ref, m_ref, acc_ref):
    tq = q_ref.shape[0]
    tk = k_ref.shape[0]
    nh = m_ref.shape[0]
    p = pl.program_id(2)
    hg = pl.program_id(1)
    qi = qi_ref[p]
    kj = kj_ref[p]

    def lanes(hh):
        return slice((hh // 2) * LANES, (hh // 2 + 1) * LANES)

    @pl.when(kj == 0)
    def _():
        lane = lax.broadcasted_iota(jnp.int32, (tq, LANES), 1)
        for hh in range(nh):
            q = q_ref[:, lanes(hh)].astype(F32)
            mine = (lane < FOX_HD) if hh % 2 == 0 else (lane >= FOX_HD)
            qm_ref[hh] = jnp.where(mine, q, 0.0).astype(BF)
        m_ref[...] = jnp.full_like(m_ref, NEG)
        acc_ref[...] = jnp.zeros_like(acc_ref)

    def step(diagonal):
        fk = fk_ref[...] * LOG2E
        head = lax.broadcasted_iota(jnp.int32, fk.shape, 1)
        for hh in range(nh):
            k = k_ref[:, lanes(hh)]
            vo = vo_ref[hh * LANES:(hh + 1) * LANES, :]
            fk_col = jnp.sum(jnp.where(head == nh * hg + hh, fk, 0.0), axis=1, keepdims=True)
            st = lax.dot_general(k, qm_ref[hh], _NT, preferred_element_type=F32) - fk_col
            if diagonal:
                krow = lax.broadcasted_iota(jnp.int32, (tk, tq), 0)
                qcol = lax.broadcasted_iota(jnp.int32, (tk, tq), 1)
                st = jnp.where(krow <= qcol, st, NEG)
            fq = fq_ref[hh:hh + 1, :] * LOG2E
            m_old = m_ref[hh]
            m_new = jnp.maximum(m_old, jnp.max(st, axis=0, keepdims=True) + fq)
            alpha = jnp.exp2(m_old - m_new)
            pm = jnp.exp2(st + (fq - m_new))
            acc_ref[hh] = alpha * acc_ref[hh] + _dot(vo, pm.astype(BF))
            m_ref[hh] = m_new

    @pl.when(kj < qi)
    def _():
        step(False)

    @pl.when(kj == qi)
    def _():
        step(True)
        for hh in range(0, nh, 2):
            even, odd = acc_ref[hh], acc_ref[hh + 1]
            ot = jnp.concatenate([even[:FOX_HD] / even[FOX_HD:FOX_HD + 1],
                                  odd[FOX_HD:] / odd[0:1]], axis=0)
            o_ref[:, lanes(hh)] = ot.T.astype(o_ref.dtype)


def _fox_prompt(q, k, vo, cum, cum_t, batch, seq, tq, nh):
    nq = seq // tq
    pairs = [(i, j) for i in range(nq) for j in range(i + 1)]
    qi_tab = jnp.asarray([p[0] for p in pairs], jnp.int32)
    kj_tab = jnp.asarray([p[1] for p in pairs], jnp.int32)
    groups = FOX_HEADS // nh
    w = nh * FOX_HD
    f_key = cum.reshape(batch, seq, FOX_HEADS)
    f_query = cum_t.reshape(batch, groups, nh, seq)
    grid_spec = pltpu.PrefetchScalarGridSpec(
        num_scalar_prefetch=2,
        grid=(batch, groups, len(pairs)),
        in_specs=[
            pl.BlockSpec((tq, w), lambda b, h, p, qi, kj: (b * nq + qi[p], h)),
            pl.BlockSpec((tq, w), lambda b, h, p, qi, kj: (b * nq + kj[p], h)),
            pl.BlockSpec((None, nh * LANES, tq), lambda b, h, p, qi, kj: (b, h, kj[p])),
            pl.BlockSpec((None, None, nh, tq), lambda b, h, p, qi, kj: (b, h, 0, qi[p])),
            pl.BlockSpec((None, tq, FOX_HEADS), lambda b, h, p, qi, kj: (b, kj[p], 0)),
        ],
        out_specs=pl.BlockSpec((tq, w), lambda b, h, p, qi, kj: (b * nq + qi[p], h)),
        scratch_shapes=[
            pltpu.VMEM((nh, tq, LANES), BF),
            pltpu.VMEM((nh, 1, tq), F32),
            pltpu.VMEM((nh, LANES, tq), F32),
        ],
    )
    return pl.pallas_call(
        _fox_prompt_kernel,
        grid_spec=grid_spec,
        out_shape=jax.ShapeDtypeStruct((batch * seq, D_MODEL), BF),
        compiler_params=_cparams("parallel", "parallel", "arbitrary"),
        name="fox_prompt_attention",
    )(qi_tab, kj_tab, q, k, vo, f_query, f_key)


def _fox_sample_kernel(pt_ref, q_ref, kn_ref, vn_ref, lfn_ref, *refs, npg, seq):
    kt_refs = refs[:npg]
    vt_refs = refs[npg:2 * npg]
    lf_refs = refs[2 * npg:3 * npg]
    o_ref = refs[3 * npg]
    qbd_ref, carry_ref, fn_ref, m_ref, l_ref, acc_ref, kpad_ref, vpad_ref = refs[3 * npg + 1:]
    g = pl.program_id(1)
    C = seq * FOX_HEADS
    P = kpad_ref.shape[0]
    row = lax.broadcasted_iota(jnp.int32, (C, D_MODEL), 0)
    col = lax.broadcasted_iota(jnp.int32, (C, D_MODEL), 1)
    own_head = (col >> 6) == (row % FOX_HEADS)
    t_of = lax.broadcasted_iota(jnp.int32, (C, P), 0) // FOX_HEADS
    j_of = lax.broadcasted_iota(jnp.int32, (C, P), 1)
    kk = lax.broadcasted_iota(jnp.int32, (P, P), 0)
    jj = lax.broadcasted_iota(jnp.int32, (P, P), 1)

    def tile_heads(a):
        return jnp.concatenate([a] * seq, axis=0)

    @pl.when(g == 0)
    def _():
        q = q_ref[...]
        qrep = jnp.concatenate(
            [jnp.broadcast_to(q[t:t + 1, :], (FOX_HEADS, D_MODEL)) for t in range(seq)], axis=0)
        qbd = jnp.where(own_head, qrep, 0.0).astype(BF)
        qbd_ref[...] = qbd
        kpad_ref[...] = jnp.zeros_like(kpad_ref)
        vpad_ref[...] = jnp.zeros_like(vpad_ref)
        kpad_ref[0:seq, :] = kn_ref[...]
        vpad_ref[0:seq, :] = vn_ref[...]
        fn = _dot_exact(lfn_ref[...], jnp.where(kk <= jj, 1.0, 0.0))
        fn_c = tile_heads(fn)
        fn_col = jnp.sum(jnp.where(j_of == t_of, fn_c, 0.0), axis=-1, keepdims=True)
        fn_ref[...] = fn_col
        s = lax.dot_general(qbd, kpad_ref[...].astype(BF), _NT, preferred_element_type=F32)
        s = jnp.where(j_of <= t_of, s + fn_col - fn_c, NEG)
        m = jnp.max(s, axis=-1, keepdims=True)
        pm = jnp.exp(s - m)
        m_ref[...] = m
        l_ref[...] = jnp.sum(pm, axis=-1, keepdims=True)
        acc_ref[...] = _dot(pm.astype(BF), vpad_ref[...].astype(BF))
        carry_ref[...] = jnp.zeros_like(carry_ref)

    later = jnp.where(kk > jj, 1.0, 0.0)
    qbd = qbd_ref[...]
    fn_col = fn_ref[...]
    carry = carry_ref[...]
    scores = []
    for i in reversed(range(npg)):
        lf = lf_refs[i][...]
        r_page = _dot_exact(lf, later) + carry
        carry = carry + jnp.sum(lf, axis=-1, keepdims=True)
        s = _dot(qbd, kt_refs[i][...].astype(BF))
        scores.append((i, s + fn_col + tile_heads(r_page)))
    carry_ref[...] = carry
    m_old = m_ref[...]
    m_new = m_old
    for _, s in scores:
        m_new = jnp.maximum(m_new, jnp.max(s, axis=-1, keepdims=True))
    alpha = jnp.exp(m_old - m_new)
    l_new = alpha * l_ref[...]
    acc = alpha * acc_ref[...]
    for i, s in scores:
        pm = jnp.exp(s - m_new)
        l_new = l_new + jnp.sum(pm, axis=-1, keepdims=True)
        acc = acc + lax.dot_general(pm.astype(BF), vt_refs[i][...].astype(BF), _NT,
                                    preferred_element_type=F32)
    m_ref[...] = m_new
    l_ref[...] = l_new
    acc_ref[...] = acc

    @pl.when(g == pl.num_programs(1) - 1)
    def _():
        on = jnp.where(own_head, acc / l_new, 0.0)
        for t in range(seq):
            o_ref[t:t + 1, :] = jnp.sum(on[t * FOX_HEADS:(t + 1) * FOX_HEADS, :], axis=0, keepdims=True)


def _fox_sample(q, k_new, v_new, lf_new_t, cache_kt, cache_vt, cache_lf_t, page_table, seq, npg):
    batch, n_pages = page_table.shape
    P = cache_kt.shape[-1]
    groups = n_pages // npg
    C = seq * FOX_HEADS

    def page(i):
        return lambda b, g, pt: (pt[b, (groups - 1 - g) * npg + i], 0, 0)

    new_tok = pl.BlockSpec((None, seq, D_MODEL), lambda b, g, pt: (b, 0, 0))
    grid_spec = pltpu.PrefetchScalarGridSpec(
        num_scalar_prefetch=1,
        grid=(batch, groups),
        in_specs=(
            [new_tok, new_tok, new_tok,
             pl.BlockSpec((None, FOX_HEADS, P), lambda b, g, pt: (b, 0, 0))]
            + [pl.BlockSpec((None, D_MODEL, P), page(i)) for i in range(npg)]
            + [pl.BlockSpec((None, D_MODEL, P), page(i)) for i in range(npg)]
            + [pl.BlockSpec((None, FOX_HEADS, P), page(i)) for i in range(npg)]
        ),
        out_specs=new_tok,
        scratch_shapes=[
            pltpu.VMEM((C, D_MODEL), BF),
            pltpu.VMEM((FOX_HEADS, 1), F32),
            pltpu.VMEM((C, 1), F32),
            pltpu.VMEM((C, 1), F32),
            pltpu.VMEM((C, 1), F32),
            pltpu.VMEM((C, D_MODEL), F32),
            pltpu.VMEM((P, D_MODEL), F32),
            pltpu.VMEM((P, D_MODEL), F32),
        ],
    )
    return pl.pallas_call(
        functools.partial(_fox_sample_kernel, npg=npg, seq=seq),
        grid_spec=grid_spec,
        out_shape=jax.ShapeDtypeStruct((batch, seq, D_MODEL), F32),
        compiler_params=_cparams("parallel", "arbitrary"),
        name="fox_sample_attention",
    )(page_table, q, k_new, v_new, lf_new_t,
      *([cache_kt] * npg), *([cache_vt] * npg), *([cache_lf_t] * npg))


def _prep_weights(w_ret_in, w_ret_out, w_q, w_o, w_ffn_in, w_ffn_out, w_kvf):
    depth, _, two_f = w_ffn_in.shape
    d_ff = two_f // 2
    nf = d_ff // FFN_CHUNK
    wi = w_ffn_in.astype(BF).reshape(depth, D_MODEL, 2 * nf, FFN_CHUNK).transpose(0, 2, 1, 3)
    wo = w_ffn_out.astype(BF).reshape(depth, nf, FFN_CHUNK, D_MODEL)
    w_kvf_b = w_kvf.astype(BF)
    wf = jnp.zeros((D_MODEL, LANES), BF).at[:, :FOX_HEADS].set(w_kvf_b[:, 2 * D_MODEL:])
    return dict(
        ret_in=w_ret_in.astype(BF), ret_out=w_ret_out.astype(BF), q=w_q.astype(BF), o=w_o.astype(BF),
        ffn_in=wi, ffn_out=wo, wk=w_kvf_b[:, :D_MODEL], wv=w_kvf_b[:, D_MODEL:2 * D_MODEL], wf=wf)


def _rope_tables(pos):
    half = RET_DK // 2
    freqs = ROPE_BASE ** (-jnp.arange(half, dtype=F32) / half)
    ang = pos.astype(F32)[:, None] * freqs[None, :]
    return jnp.cos(ang), jnp.sin(ang)


def _trunk(x, mods, kvmod, mod, pos, W, norms, b_f, *, tm, kvf_seq, ret_fn, attn_fn):
    ret_norm_g, attn_norm_g, q_norm_g, ffn_norm_g, kv_norm_g, k_norm_g = norms
    n_a = W["ret_in"].shape[0]
    depth = W["ffn_in"].shape[0]
    log_g = jnp.log1p(-jnp.exp2(-5.0 - jnp.arange(RET_HEADS, dtype=F32)))
    cos, sin = _rope_tables(pos)
    states = []
    shared = None
    for l in range(depth):
        sh1, sc1, g1, sh2, sc2, g2 = mods[l]
        if l < n_a:
            a, S = ret_fn(l, x, ret_norm_g[l], sh1, sc1, cos, sin, log_g, states)
            states.append(S)
            x = _out_proj(a, W["ret_out"], l, x, g1, mod, tm)
        else:
            j = l - n_a
            if shared is None:
                shared = _kvf_proj(x, kv_norm_g, kvmod[0], kvmod[1], W["wk"], W["wv"], W["wf"],
                                   b_f, k_norm_g, mod, min(tm, 512), kvf_seq)
            a = attn_fn(j, x, attn_norm_g[j], sh1, sc1, q_norm_g[j], shared)
            x = _out_proj(a, W["o"], j, x, g1, mod, tm)
        x = _ffn(x, ffn_norm_g[l], sh2, sc2, g2, W["ffn_in"], W["ffn_out"], l, mod, tm)
    stacked = states[-1] if states[-1].ndim == 5 else jnp.stack(states)
    return x, stacked, shared


def kernel(x_prompt, x_sample, cache_k, cache_v, cache_logf, state_ret, page_table, c_prompt, c_sample,
           w_mod, b_mod, ret_norm_g, w_ret_in, w_ret_out, attn_norm_g, w_q, q_norm_g, w_o, ffn_norm_g,
           w_ffn_in, w_ffn_out, kv_norm_g, w_kvmod, b_kvmod, w_kvf, b_f, k_norm_g):
    Bp, Tp, _ = x_prompt.shape
    Bd, Td, _ = x_sample.shape
    depth = w_mod.shape[0]
    n_pool, page_size = cache_k.shape[:2]
    past_len = page_table.shape[1] * page_size
    W = _prep_weights(w_ret_in, w_ret_out, w_q, w_o, w_ffn_in, w_ffn_out, w_kvf)
    norms = (ret_norm_g, attn_norm_g, q_norm_g, ffn_norm_g, kv_norm_g, k_norm_g)

    c_all = jnp.concatenate([c_prompt, c_sample], axis=0)
    mod_all = _cond_proj(c_all, w_mod, b_mod, 1536)
    kvmod_all = _cond_proj(c_all, w_kvmod[None], b_kvmod[None], 1024)[0]

    def split_mods(lo, hi, expand):
        def lay(a):
            return jnp.repeat(a, expand, axis=0)[None] if expand else a[:, None, :]
        mods = [[lay(m) for m in jnp.split(mod_all[l, lo:hi], 6, axis=-1)] for l in range(depth)]
        kvmod = [lay(m) for m in jnp.split(kvmod_all[lo:hi], 2, axis=-1)]
        return mods, kvmod

    mods_p, kvmod_p = split_mods(0, Bp, 0)
    mod_p = _Mod(Tp, per_row=False)
    tm_p = 1024

    def ret_prompt(l, x, g, sh, sc, cos, sin, log_g, _):
        proj = _nm_matmul(x, g, sh, sc, W["ret_in"], l, mod_p, tm_p, 3072, BF)
        return _ret_prompt(proj, cos, sin, log_g, Bp, Tp, 2048)

    def attn_prompt(j, x, g, sh, sc, gq, shared):
        _, _, _, cum_t, cum, kb, vo = shared
        q = _q_proj(x, g, sh, sc, W["q"], j, gq, mod_p, tm_p, BF, FOX_HD ** -0.5 * LOG2E)
        return _fox_prompt(q, kb, vo, cum, cum_t, Bp, Tp, 1024, 8)

    y_p, st_p, (kt_p, vt_p, lft_p, _, _, _, _) = _trunk(
        x_prompt.reshape(Bp * Tp, D_MODEL), mods_p, kvmod_p, mod_p, jnp.arange(Tp), W, norms, b_f,
        tm=tm_p, kvf_seq=Tp, ret_fn=ret_prompt, attn_fn=attn_prompt)

    mods_s, kvmod_s = split_mods(Bp, Bp + Bd, Td)
    Ms = Bd * Td
    mod_s = _Mod(Ms, per_row=True)
    ckt = cache_k.transpose(0, 2, 3, 1).reshape(n_pool, D_MODEL, page_size)
    cvt = cache_v.transpose(0, 2, 3, 1).reshape(n_pool, D_MODEL, page_size)
    clf_t = cache_logf.transpose(0, 2, 1)

    def ret_sample(l, x, g, sh, sc, cos, sin, log_g, states):
        proj = _nm_matmul(x, g, sh, sc, W["ret_in"], l, mod_s, Ms, 1536, F32)
        return _ret_sample(proj, state_ret, l, states[-1] if states else None, cos, sin, log_g, Td, 2)

    def attn_sample(j, x, g, sh, sc, gq, shared):
        k, v, lf = shared
        q = _q_proj(x, g, sh, sc, W["q"], j, gq, mod_s, Ms, F32, FOX_HD ** -0.5)
        lf_t = jnp.pad(lf.reshape(Bd, Td, FOX_HEADS).transpose(0, 2, 1),
                       ((0, 0), (0, 0), (0, page_size - Td)))
        o = _fox_sample(q.reshape(Bd, Td, D_MODEL), k.reshape(Bd, Td, D_MODEL), v.reshape(Bd, Td, D_MODEL),
                        lf_t, ckt, cvt, clf_t, page_table, Td, 16)
        return o.reshape(Ms, D_MODEL)

    y_s, st_s, (k_s, v_s, lf_s) = _trunk(
        x_sample.reshape(Ms, D_MODEL), mods_s, kvmod_s, mod_s, past_len + jnp.arange(Td), W, norms, b_f,
        tm=Ms, kvf_seq=None, ret_fn=ret_sample, attn_fn=attn_sample)

    heads = (FOX_HEADS, FOX_HD)
    k_p = kt_p.reshape(Bp, *heads, Tp).transpose(0, 3, 1, 2)
    v_p = vt_p.reshape(Bp, *heads, Tp).transpose(0, 3, 1, 2)
    return (y_p.reshape(Bp, Tp, D_MODEL), y_s.reshape(Bd, Td, D_MODEL), st_p, st_s,
            k_p, v_p, lft_p.transpose(0, 2, 1),
            k_s.reshape(Bd, Td, *heads), v_s.reshape(Bd, Td, *heads), lf_s.reshape(Bd, Td, FOX_HEADS))
```
